```python
import jax, jax.numpy as jnp
from jax import lax
import numpy as np

D_MODEL = 2048
BATCH = 2
SEQ = 4096
DEPTH = 2

CTX_LEN = 256
GRID_W = 64
D_MIX = D_MODEL
CONV_CH = D_MIX // 2
CONV_K = 31
HG_HEADS = 8
HG_DK = 128
HG_DV = 128
HG_KEY = HG_HEADS * HG_DK
HG_WIDTH = HG_HEADS * HG_DV
HG_CHUNK = 64
REC_PROJ = 3 * HG_KEY + 2 * HG_WIDTH
D_PROJ = 2 * CONV_CH + REC_PROJ
D_FF_DENSE = 5632
N_EXPERTS = 8
TOP_K = 2
D_FF_EXPERT = 7168
MOE_BLOCK = 256
N_DENSE = (DEPTH + 1) // 2
N_MOE = DEPTH // 2
EPS = 1e-6
F32 = jnp.float32

kernel_name = 'hybrid_conformer_hgrn2_moe_dit'


def rmsnorm(x, g):
    xf = x.astype(F32)
    y = xf * lax.rsqrt(jnp.mean(xf * xf, axis=-1, keepdims=True) + EPS)
    return (y * g.astype(F32)).astype(x.dtype)


def layernorm(x, g, b):
    xf = x.astype(F32)
    mu = jnp.mean(xf, axis=-1, keepdims=True)
    var = jnp.mean(jnp.square(xf - mu), axis=-1, keepdims=True)
    return ((xf - mu) * lax.rsqrt(var + EPS) * g.astype(F32) + b.astype(F32)).astype(x.dtype)


def modulate(h, shift, scale):
    return h * (1 + scale) + shift


def swiglu(h, w1, w3, w2):
    return (jax.nn.silu(h @ w1) * (h @ w3)) @ w2


def depthwise_conv(u, w, b):
    y = lax.conv_general_dilated(u, w[:, None, :], window_strides=(1,),
                                 padding=[(CONV_K // 2, CONV_K // 2)],
                                 dimension_numbers=('NWC', 'WIO', 'NWC'),
                                 feature_group_count=u.shape[-1])
    return y + b


def conformer_conv(p, w, b, ln_g, ln_b, n_seg, seg_len):
    val, gate = jnp.split(p, 2, axis=-1)
    u = val * jax.nn.sigmoid(gate)
    bsz, t, ch = u.shape
    y = depthwise_conv(u.reshape(bsz * n_seg, seg_len, ch), w, b).reshape(bsz, t, ch)
    return jax.nn.silu(layernorm(y, ln_g, ln_b))


def hgrn_lower_bounds(logits):
    cs = jnp.cumsum(jax.nn.softmax(logits.astype(F32), axis=1), axis=1)
    return cs - cs[:, :1]


def hgrn_chunk_scan(q, k, v, logf, s0):
    bsz, t, h, dk = q.shape
    dv = v.shape[-1]
    n = t // HG_CHUNK

    def to_chunks(a):
        return a.reshape(bsz, n, HG_CHUNK, h, a.shape[-1]).transpose(1, 0, 3, 2, 4)

    incl = jnp.tril(jnp.ones((HG_CHUNK, HG_CHUNK), dtype=bool))[:, :, None]

    def step(s, inp):
        qc, kc, vc, lf = inp
        b = jnp.cumsum(lf, axis=2)
        o_inter = jnp.einsum('bhtk,bhkv->bhtv', qc * jnp.exp(b), s)
        rel = b[:, :, :, None, :] - b[:, :, None, :, :]
        decay = jnp.exp(jnp.where(incl, rel, -jnp.inf))
        scores = jnp.einsum('bhtk,bhsk,bhtsk->bhts', qc, kc, decay)
        o = o_inter + jnp.einsum('bhts,bhsv->bhtv', scores, vc)
        b_last = b[:, :, -1:, :]
        s_new = jnp.exp(b_last[:, :, 0, :])[..., None] * s + jnp.einsum('bhsk,bhsv->bhkv', kc * jnp.exp(b_last - b), vc)
        return s_new, o

    s_fin, o = lax.scan(step, s0, (to_chunks(q), to_chunks(k), to_chunks(v), to_chunks(logf)))
    return o.transpose(1, 0, 3, 2, 4).reshape(bsz, t, h, dv), s_fin


def hgrn2_mixer(p_lat, p_ctx, lb, norm_g, need_ctx_out):
    def heads(a, d):
        return a.reshape(a.shape[0], a.shape[1], HG_HEADS, d).astype(F32)

    def forget(z, lb_dir):
        f = lb_dir + (1 - lb_dir) * jax.nn.sigmoid(z.astype(F32))
        return heads(jnp.log(f), HG_DK), heads(1 - f, HG_DK)

    def run(p, s0_f, s0_b):
        q, z_f, z_b, i, g = jnp.split(p, [HG_KEY, 2 * HG_KEY, 3 * HG_KEY, 3 * HG_KEY + HG_WIDTH], axis=-1)
        qh, ih = heads(q, HG_DK), heads(i, HG_DV)
        lf_f, k_f = forget(z_f, lb[0])
        lf_b, k_b = forget(z_b, lb[1])
        o_f, s_f = hgrn_chunk_scan(qh, k_f, ih, lf_f, s0_f)
        flip = lambda a: jnp.flip(a, axis=1)
        o_b, s_b = hgrn_chunk_scan(flip(qh), flip(k_b), flip(ih), flip(lf_b), s0_b)
        return o_f + flip(o_b), s_f, s_b, g

    def readout(o, g, dtype):
        o = o * lax.rsqrt(jnp.mean(o * o, axis=-1, keepdims=True) + EPS) * norm_g.astype(F32).reshape(HG_HEADS, HG_DV)
        return (o.reshape(o.shape[0], o.shape[1], HG_WIDTH) * jax.nn.silu(g.astype(F32))).astype(dtype)

    s0 = jnp.zeros((p_lat.shape[0], HG_HEADS, HG_DK, HG_DV), F32)
    o_c, sc_f, sc_b, g_c = run(p_ctx, s0, s0)
    o_x, _, _, g_x = run(p_lat, sc_f, sc_b)
    y_lat = readout(o_x, g_x, p_lat.dtype)
    y_ctx = readout(o_c, g_c, p_ctx.dtype) if need_ctx_out else None
    return y_lat, y_ctx


def moe_swiglu(h, w_router, w1, w3, w2):
    t = h.reshape(-1, h.shape[-1])
    n = t.shape[0]
    logits = t.astype(F32) @ w_router.astype(F32)
    top_logit, top_e = lax.top_k(logits, TOP_K)
    top_w = jax.nn.softmax(top_logit, axis=-1)
    a_e = top_e.reshape(-1)
    a_tok = jnp.repeat(jnp.arange(n, dtype=jnp.int32), TOP_K)
    a_w = top_w.reshape(-1)
    order = jnp.argsort(a_e)
    e_s, tok_s, w_s = a_e[order], a_tok[order], a_w[order]
    counts = jnp.bincount(a_e, length=N_EXPERTS)
    padded = (counts + MOE_BLOCK - 1) // MOE_BLOCK * MOE_BLOCK
    pad_end = jnp.cumsum(padded)
    pad_start = pad_end - padded
    start = jnp.cumsum(counts) - counts
    dest = pad_start[e_s] + (jnp.arange(a_e.shape[0], dtype=jnp.int32) - start[e_s])
    n_blocks = (n * TOP_K + N_EXPERTS * (MOE_BLOCK - 1) + MOE_BLOCK - 1) // MOE_BLOCK
    buf = jnp.zeros((n_blocks * MOE_BLOCK, t.shape[1]), t.dtype).at[dest].set(t[tok_s])
    blk_e = jnp.minimum(jnp.searchsorted(pad_end, jnp.arange(n_blocks) * MOE_BLOCK, side='right'), N_EXPERTS - 1)

    def expert_block(args):
        xb, e = args
        return swiglu(xb, w1[e], w3[e], w2[e])

    y = lax.map(expert_block, (buf.reshape(n_blocks, MOE_BLOCK, t.shape[1]), blk_e)).reshape(-1, t.shape[1])
    out = jnp.zeros_like(t).at[tok_s].add(y[dest] * w_s[:, None].astype(y.dtype))
    return out.reshape(h.shape)


def channel_mixer(h, l, ffn_w1, ffn_w3, ffn_w2, router_w, moe_w1, moe_w3, moe_w2):
    if l % 2 == 0:
        return swiglu(h, ffn_w1[l // 2], ffn_w3[l // 2], ffn_w2[l // 2])
    return moe_swiglu(h, router_w[l // 2], moe_w1[l // 2], moe_w3[l // 2], moe_w2[l // 2])


def setup_inputs(seed: int = 0) -> dict:
    key = jax.random.key(seed)
    ks = jax.random.split(key, 24)

    def nrm(k, shape, scale):
        return scale * jax.random.normal(k, shape, F32)

    return {
        'x': nrm(ks[0], (BATCH, SEQ, D_MODEL), 1.0),
        'c': nrm(ks[1], (BATCH, D_MODEL), 1.0),
        'ctx': nrm(ks[2], (BATCH, CTX_LEN, D_MODEL), 1.0),
        'c_ctx': nrm(ks[3], (D_MODEL,), 1.0),
        'w_ada': nrm(ks[4], (DEPTH, D_MODEL, 6 * D_MODEL), 0.5 * D_MODEL ** -0.5),
        'b_ada': nrm(ks[5], (DEPTH, 6 * D_MODEL), 0.02),
        'g_mix': 1.0 + nrm(ks[6], (DEPTH, D_MODEL), 0.1),
        'w_in': nrm(ks[7], (DEPTH, D_MODEL, D_PROJ), D_MODEL ** -0.5),
        'conv_w': nrm(ks[8], (DEPTH, CONV_K, CONV_CH), CONV_K ** -0.5),
        'conv_b': nrm(ks[9], (DEPTH, CONV_CH), 0.02),
        'conv_ln_g': 1.0 + nrm(ks[10], (DEPTH, CONV_CH), 0.1),
        'conv_ln_b': nrm(ks[11], (DEPTH, CONV_CH), 0.02),
        'hgrn_lb_logits': nrm(ks[12], (2, DEPTH, HG_KEY), 0.5),
        'hgrn_norm_g': 1.0 + nrm(ks[13], (DEPTH, HG_WIDTH), 0.1),
        'w_out': nrm(ks[14], (DEPTH, D_MIX, D_MODEL), D_MIX ** -0.5),
        'g_ffn': 1.0 + nrm(ks[15], (DEPTH, D_MODEL), 0.1),
        'ffn_w1': nrm(ks[16], (N_DENSE, D_MODEL, D_FF_DENSE), D_MODEL ** -0.5),
        'ffn_w3': nrm(ks[17], (N_DENSE, D_MODEL, D_FF_DENSE), D_MODEL ** -0.5),
        'ffn_w2': nrm(ks[18], (N_DENSE, D_FF_DENSE, D_MODEL), D_FF_DENSE ** -0.5),
        'router_w': nrm(ks[19], (N_MOE, D_MODEL, N_EXPERTS), D_MODEL ** -0.5),
        'moe_w1': nrm(ks[20], (N_MOE, N_EXPERTS, D_MODEL, D_FF_EXPERT), D_MODEL ** -0.5),
        'moe_w3': nrm(ks[21], (N_MOE, N_EXPERTS, D_MODEL, D_FF_EXPERT), D_MODEL ** -0.5),
        'moe_w2': nrm(ks[22], (N_MOE, N_EXPERTS, D_FF_EXPERT, D_MODEL), D_FF_EXPERT ** -0.5),
        'g_final': 1.0 + nrm(ks[23], (D_MODEL,), 0.1),
    }


def reference(x, c, ctx, c_ctx, w_ada, b_ada, g_mix, w_in, conv_w, conv_b, conv_ln_g, conv_ln_b,
              hgrn_lb_logits, hgrn_norm_g, w_out, g_ffn, ffn_w1, ffn_w3, ffn_w2, router_w,
              moe_w1, moe_w3, moe_w2, g_final):
    rows = x.shape[1] // GRID_W
    lb_all = hgrn_lower_bounds(hgrn_lb_logits)
    for l in range(DEPTH):
        last = l == DEPTH - 1
        mod_x = jnp.split((jax.nn.silu(c) @ w_ada[l] + b_ada[l])[:, None, :], 6, axis=-1)
        mod_c = jnp.split(jax.nn.silu(c_ctx) @ w_ada[l] + b_ada[l], 6, axis=-1)

        hx = modulate(rmsnorm(x, g_mix[l]), mod_x[0], mod_x[1])
        hc = modulate(rmsnorm(ctx, g_mix[l]), mod_c[0], mod_c[1])
        px = hx @ w_in[l]
        pc = hc @ (w_in[l] if not last else w_in[l][:, 2 * CONV_CH:])
        rec_x, rec_c = hgrn2_mixer(px[..., -REC_PROJ:], pc[..., -REC_PROJ:], lb_all[:, l], hgrn_norm_g[l], not last)
        conv_x = conformer_conv(px[..., :2 * CONV_CH], conv_w[l], conv_b[l], conv_ln_g[l], conv_ln_b[l], rows, GRID_W)
        x = x + mod_x[2] * (jnp.concatenate([conv_x, rec_x], axis=-1) @ w_out[l])
        if not last:
            conv_c = conformer_conv(pc[..., :2 * CONV_CH], conv_w[l], conv_b[l], conv_ln_g[l], conv_ln_b[l], 1, pc.shape[1])
            ctx = ctx + mod_c[2] * (jnp.concatenate([conv_c, rec_c], axis=-1) @ w_out[l])

        hx = modulate(rmsnorm(x, g_ffn[l]), mod_x[3], mod_x[4])
        x = x + mod_x[5] * channel_mixer(hx, l, ffn_w1, ffn_w3, ffn_w2, router_w, moe_w1, moe_w3, moe_w2)
        if not last:
            hc = modulate(rmsnorm(ctx, g_ffn[l]), mod_c[3], mod_c[4])
            ctx = ctx + mod_c[5] * channel_mixer(hc, l, ffn_w1, ffn_w3, ffn_w2, router_w, moe_w1, moe_w3, moe_w2)
    return rmsnorm(x, g_final)
```

```python
import functools

import jax
import jax.numpy as jnp
from jax import lax
from jax.experimental import pallas as pl
from jax.experimental.pallas import tpu as pltpu

F32 = jnp.float32
BF16 = jnp.bfloat16
EPS = 1e-6

GRID_W = 64
CONV_K = 31
CONV_PAD = 16
HG_HEADS = 8
HG_D = 128
HG_CHUNK = 128
N_EXPERTS = 8
TOP_K = 2
MOE_ROWS = 512
ROW_TILE = 512
OUT_ROW_TILE = 256
MOD_ROWS = 8
V7X_VMEM_LIMIT = 56 * 1024 * 1024

_NT = (((1,), (1,)), ((), ()))
_TN = (((0,), (0,)), ((), ()))


def _cparams(sem):
    return pltpu.CompilerParams(dimension_semantics=sem, vmem_limit_bytes=V7X_VMEM_LIMIT)


def _mod_row(i, tm, layout):
    ctx_rows, seq = layout
    assert ctx_rows % tm == 0 and seq % tm == 0
    return jnp.where(i < ctx_rows // tm, 2, (i - ctx_rows // tm) // (seq // tm))


def _norm_mod(x, g, shift, scale):
    y = x * lax.rsqrt(jnp.mean(x * x, axis=-1, keepdims=True) + EPS) * g
    return y * (1.0 + scale) + shift


def _ada_kernel(c_ref, w_ref, b_ref, o_ref):
    c = c_ref[...]
    a = c * jax.nn.sigmoid(c)
    o_ref[0] = jnp.dot(a, w_ref[0], preferred_element_type=F32,
                       precision=lax.Precision.HIGHEST) + b_ref[0]


def _adaln(cc, w_ada, b_ada, tn=1024):
    depth, d, n = w_ada.shape
    return pl.pallas_call(
        _ada_kernel,
        out_shape=jax.ShapeDtypeStruct((depth, MOD_ROWS, n), F32),
        grid=(depth, n // tn),
        in_specs=[pl.BlockSpec((MOD_ROWS, d), lambda l, j: (0, 0)),
                  pl.BlockSpec((1, d, tn), lambda l, j: (l, 0, j)),
                  pl.BlockSpec((1, 1, tn), lambda l, j: (l, 0, j))],
        out_specs=pl.BlockSpec((1, MOD_ROWS, tn), lambda l, j: (l, 0, j)),
        compiler_params=_cparams(("parallel", "parallel")),
        name="adaln",
    )(cc, w_ada, b_ada.reshape(depth, 1, n))


def _proj_kernel(x_ref, mod_ref, g_ref, w_ref, o_ref, h_scr):
    @pl.when(pl.program_id(1) == 0)
    def _():
        h = _norm_mod(x_ref[...], g_ref[...], mod_ref[0, 0:1, :], mod_ref[0, 1:2, :])
        h_scr[...] = h.astype(BF16)

    o_ref[...] = jnp.dot(h_scr[...], w_ref[...], preferred_element_type=F32)


def _in_proj(tok, mod, g, w, layout, tn=1024):
    rows, d = tok.shape
    n = w.shape[1]
    tm = ROW_TILE
    return pl.pallas_call(
        _proj_kernel,
        out_shape=jax.ShapeDtypeStruct((rows, n), F32),
        grid=(rows // tm, n // tn),
        in_specs=[pl.BlockSpec((tm, d), lambda i, j: (i, 0)),
                  pl.BlockSpec((1, 6, d), lambda i, j: (_mod_row(i, tm, layout), 0, 0)),
                  pl.BlockSpec((1, d), lambda i, j: (0, 0)),
                  pl.BlockSpec((d, tn), lambda i, j: (0, j))],
        out_specs=pl.BlockSpec((tm, tn), lambda i, j: (i, j)),
        scratch_shapes=[pltpu.VMEM((tm, d), BF16)],
        compiler_params=_cparams(("parallel", "arbitrary")),
        name="in_proj",
    )(tok, mod, g, w)


def _conv_segment(u, w_ref, pad_scr, seg_len):
    ch = u.shape[1]
    zeros = jnp.zeros((CONV_PAD, ch), F32)
    pad_scr[0:CONV_PAD, :] = zeros
    pad_scr[CONV_PAD + seg_len:2 * CONV_PAD + seg_len, :] = zeros
    pad_scr[CONV_PAD:CONV_PAD + seg_len, :] = u
    base = CONV_PAD - CONV_K // 2
    acc = jnp.zeros((seg_len, ch), F32)
    for k in range(CONV_K):
        acc = acc + pad_scr[base + k:base + k + seg_len, :] * w_ref[k:k + 1, :]
    return acc


def _conv_kernel(val_ref, gate_ref, w_ref, b_ref, lg_ref, lb_ref, o_ref, pad_scr, *,
                 ctx_blocks, ctx_len):
    rows = val_ref.shape[0]

    def run(seg_len):
        for s in range(rows // seg_len):
            sl = slice(s * seg_len, (s + 1) * seg_len)
            u = val_ref[sl, :] * jax.nn.sigmoid(gate_ref[sl, :])
            y = _conv_segment(u, w_ref, pad_scr, seg_len) + b_ref[...]
            mu = jnp.mean(y, axis=-1, keepdims=True)
            yc = y - mu
            var = jnp.mean(yc * yc, axis=-1, keepdims=True)
            z = yc * lax.rsqrt(var + EPS) * lg_ref[...] + lb_ref[...]
            o_ref[sl, :] = (z * jax.nn.sigmoid(z)).astype(o_ref.dtype)

    if ctx_blocks:
        is_ctx = pl.program_id(0) < ctx_blocks
        pl.when(is_ctx)(lambda: run(ctx_len))
        pl.when(jnp.logical_not(is_ctx))(lambda: run(GRID_W))
    else:
        run(GRID_W)


def _conformer_conv(px, w, b, lg, lb, ctx_len, with_ctx, skip_blocks):
    rows = px.shape[0]
    ch = w.shape[1]
    tm = ctx_len
    off = 0 if with_ctx else skip_blocks
    kern = functools.partial(_conv_kernel, ctx_blocks=skip_blocks if with_ctx else 0, ctx_len=ctx_len)
    vec = lambda a: a.reshape(1, ch)
    return pl.pallas_call(
        kern,
        out_shape=jax.ShapeDtypeStruct((rows, ch), BF16),
        grid=(rows // tm - off,),
        in_specs=[pl.BlockSpec((tm, ch), lambda i: (i + off, 0)),
                  pl.BlockSpec((tm, ch), lambda i: (i + off, 1)),
                  pl.BlockSpec((CONV_K, ch), lambda i: (0, 0)),
                  pl.BlockSpec((1, ch), lambda i: (0, 0)),
                  pl.BlockSpec((1, ch), lambda i: (0, 0)),
                  pl.BlockSpec((1, ch), lambda i: (0, 0))],
        out_specs=pl.BlockSpec((tm, ch), lambda i: (i + off, 0)),
        scratch_shapes=[pltpu.VMEM((tm + 2 * CONV_PAD, ch), F32)],
        compiler_params=_cparams(("parallel",)),
        name="conformer_conv",
    )(px, px, w, vec(b), vec(lg), vec(lb))


def _hgrn_kernel(q_ref, z_ref, v_ref, lb_ref, o_ref, st_scr, f_scr, b_scr, r_scr):
    c = q_ref.shape[0]
    d = pl.program_id(0)

    @pl.when(pl.program_id(2) == 0)
    def _():
        st_scr[...] = jnp.zeros_like(st_scr)

    ti = lax.broadcasted_iota(jnp.int32, (c, c), 0)
    si = lax.broadcasted_iota(jnp.int32, (c, c), 1)
    fwd = d == 0
    pt = jnp.where(fwd, ti, c - 1 - ti)
    ps = jnp.where(fwd, si, c - 1 - si)

    lb = lb_ref[0]
    f = lb + (1.0 - lb) * jax.nn.sigmoid(z_ref[...])
    f_scr[...] = f
    b_scr[...] = jnp.dot((ps <= pt).astype(F32), jnp.log(f), preferred_element_type=F32,
                         precision=lax.Precision.HIGHEST)

    nlev = r_scr.shape[0]
    masks = []
    for li in range(nlev):
        sh = nlev - 1 - li
        m = 1 << sh
        blk_t = lax.shift_right_logical(pt, sh + 1)
        blk_s = lax.shift_right_logical(ps, sh + 1)
        p_ref = lax.shift_left(blk_t, sh + 1) + (m - 1)
        sel = (ps == p_ref).astype(F32)
        r_scr[li] = jnp.dot(sel, b_scr[...], preferred_element_type=F32,
                            precision=lax.Precision.HIGHEST)
        later_t = (lax.shift_right_logical(pt, sh) & 1) == 1
        earlier_s = (lax.shift_right_logical(ps, sh) & 1) == 0
        masks.append((blk_t == blk_s) & later_t & earlier_s)

    for h in range(q_ref.shape[1] // HG_D):
        sl = slice(h * HG_D, (h + 1) * HG_D)
        qh = q_ref[:, sl]
        vh = v_ref[:, sl]
        kh = 1.0 - f_scr[:, sl]
        bh = b_scr[:, sl]
        b_last = jnp.where(fwd, bh[c - 1:c, :], bh[0:1, :])
        st = st_scr[h]
        vb = vh.astype(BF16)

        o = lax.dot_general((qh * jnp.exp(bh)).astype(BF16), st.astype(BF16), _NT,
                            preferred_element_type=F32)
        scores = jnp.zeros((c, c), F32)
        for li in range(nlev):
            dlt = bh - r_scr[li, :, sl]
            qm = (qh * jnp.exp(jnp.minimum(dlt, 0.0))).astype(BF16)
            km = (kh * jnp.exp(jnp.minimum(-dlt, 0.0))).astype(BF16)
            sc = lax.dot_general(qm, km, _NT, preferred_element_type=F32)
            scores = scores + jnp.where(masks[li], sc, 0.0)
        o = o + jnp.dot(scores.astype(BF16), vb, preferred_element_type=F32)
        o = o + jnp.sum(qh * kh, axis=-1, keepdims=True) * vh
        o_ref[0, :, sl] = o

        ke = (kh * jnp.exp(b_last - bh)).astype(BF16)
        st_scr[h] = st * jnp.exp(b_last) + lax.dot_general(vb, ke, _TN, preferred_element_type=F32)


def _hgrn_scan(px, lb, batch, ctx_len, seq_len, col0):
    rows = px.shape[0]
    c = HG_CHUNK
    w = HG_HEADS * HG_D
    cb = col0 // w
    ncx, nl = ctx_len // c, seq_len // c
    lat0 = batch * ncx

    def row_block(d, b, n):
        in_ctx = n < ncx
        nc = jnp.where(d == 0, n, ncx - 1 - n)
        nx = jnp.where(d == 0, n - ncx, nl - 1 - (n - ncx))
        return jnp.where(in_ctx, b * ncx + nc, lat0 + b * nl + nx)

    nlev = c.bit_length() - 1
    return pl.pallas_call(
        _hgrn_kernel,
        out_shape=jax.ShapeDtypeStruct((2, rows, w), F32),
        grid=(2, batch, ncx + nl),
        in_specs=[pl.BlockSpec((c, w), lambda d, b, n: (row_block(d, b, n), cb)),
                  pl.BlockSpec((c, w), lambda d, b, n: (row_block(d, b, n), cb + 1 + d)),
                  pl.BlockSpec((c, w), lambda d, b, n: (row_block(d, b, n), cb + 3)),
                  pl.BlockSpec((1, 1, w), lambda d, b, n: (d, 0, 0))],
        out_specs=pl.BlockSpec((1, c, w), lambda d, b, n: (d, row_block(d, b, n), 0)),
        scratch_shapes=[pltpu.VMEM((HG_HEADS, HG_D, HG_D), F32),
                        pltpu.VMEM((c, w), F32),
                        pltpu.VMEM((c, w), F32),
                        pltpu.VMEM((nlev, c, w), F32)],
        compiler_params=_cparams(("parallel", "parallel", "arbitrary")),
        name="hgrn_scan",
    )(px, px, px, lb.reshape(2, 1, w))


def _out_kernel(x_ref, mod_ref, conv_ref, o_ref, g_ref, ng_ref, w_ref, y_ref):
    o = o_ref[0] + o_ref[1]
    parts = []
    for h in range(HG_HEADS):
        oh = o[:, h * HG_D:(h + 1) * HG_D]
        parts.append(oh * lax.rsqrt(jnp.mean(oh * oh, axis=-1, keepdims=True) + EPS))
    g = g_ref[...]
    rec = jnp.concatenate(parts, axis=-1) * ng_ref[...] * (g * jax.nn.sigmoid(g))
    ch = conv_ref.shape[1]
    acc = jnp.dot(conv_ref[...], w_ref[0:ch, :], preferred_element_type=F32)
    acc = acc + jnp.dot(rec.astype(BF16), w_ref[ch:, :], preferred_element_type=F32)
    y_ref[...] = x_ref[...] + mod_ref[0, 2:3, :] * acc


def _out_proj(tok, mod, conv, o, px, norm_g, w, layout, skip_rows):
    rows, d = tok.shape
    ch = conv.shape[1]
    wr = o.shape[2]
    gcol = px.shape[1] // wr - 1
    tm = OUT_ROW_TILE
    off = skip_rows // tm
    return pl.pallas_call(
        _out_kernel,
        out_shape=jax.ShapeDtypeStruct((rows, d), F32),
        grid=(rows // tm - off,),
        in_specs=[pl.BlockSpec((tm, d), lambda i: (i + off, 0)),
                  pl.BlockSpec((1, 6, d), lambda i: (_mod_row(i + off, tm, layout), 0, 0)),
                  pl.BlockSpec((tm, ch), lambda i: (i + off, 0)),
                  pl.BlockSpec((2, tm, wr), lambda i: (0, i + off, 0)),
                  pl.BlockSpec((tm, wr), lambda i: (i + off, gcol)),
                  pl.BlockSpec((1, wr), lambda i: (0, 0)),
                  pl.BlockSpec((ch + wr, d), lambda i: (0, 0))],
        out_specs=pl.BlockSpec((tm, d), lambda i: (i + off, 0)),
        compiler_params=_cparams(("parallel",)),
        name="out_proj",
    )(tok, mod, conv, o, px, norm_g.reshape(1, wr), w)


def _ffn_kernel(x_ref, mod_ref, g_ref, w1_ref, w3_ref, w2_ref, y_ref, h_scr, acc_scr):
    f = pl.program_id(1)

    @pl.when(f == 0)
    def _():
        h = _norm_mod(x_ref[...], g_ref[...], mod_ref[0, 3:4, :], mod_ref[0, 4:5, :])
        h_scr[...] = h.astype(BF16)
        acc_scr[...] = jnp.zeros_like(acc_scr)

    h = h_scr[...]
    a = jnp.dot(h, w1_ref[...], preferred_element_type=F32)
    b = jnp.dot(h, w3_ref[...], preferred_element_type=F32)
    u = (a * jax.nn.sigmoid(a) * b).astype(BF16)
    acc_scr[...] += jnp.dot(u, w2_ref[...], preferred_element_type=F32)

    @pl.when(f == pl.num_programs(1) - 1)
    def _():
        y_ref[...] = x_ref[...] + mod_ref[0, 5:6, :] * acc_scr[...]


def _dense_ffn(tok, mod, g, w1, w3, w2, layout, tf=512):
    rows, d = tok.shape
    ff = w1.shape[1]
    tm = ROW_TILE
    return pl.pallas_call(
        _ffn_kernel,
        out_shape=jax.ShapeDtypeStruct((rows, d), F32),
        grid=(rows // tm, ff // tf),
        in_specs=[pl.BlockSpec((tm, d), lambda i, f: (i, 0)),
                  pl.BlockSpec((1, 6, d), lambda i, f: (_mod_row(i, tm, layout), 0, 0)),
                  pl.BlockSpec((1, d), lambda i, f: (0, 0)),
                  pl.BlockSpec((d, tf), lambda i, f: (0, f)),
                  pl.BlockSpec((d, tf), lambda i, f: (0, f)),
                  pl.BlockSpec((tf, d), lambda i, f: (f, 0))],
        out_specs=pl.BlockSpec((tm, d), lambda i, f: (i, 0)),
        scratch_shapes=[pltpu.VMEM((tm, d), BF16), pltpu.VMEM((tm, d), F32)],
        compiler_params=_cparams(("parallel", "arbitrary")),
        name="dense_ffn",
    )(tok, mod, g, w1, w3, w2)


def _router_kernel(x_ref, mod_ref, g_ref, wr_ref, h_ref, lg_ref):
    h = _norm_mod(x_ref[...], g_ref[...], mod_ref[0, 3:4, :], mod_ref[0, 4:5, :])
    h_ref[...] = h.astype(BF16)
    lg_ref[...] = jnp.dot(h, wr_ref[...], preferred_element_type=F32,
                          precision=lax.Precision.HIGHEST)


def _router(tok, mod, g, wr_pad, layout, skip_rows):
    rows, d = tok.shape
    tm = ROW_TILE
    off = skip_rows // tm
    nrow = rows - skip_rows
    ne = wr_pad.shape[1]
    return pl.pallas_call(
        _router_kernel,
        out_shape=(jax.ShapeDtypeStruct((nrow, d), BF16), jax.ShapeDtypeStruct((nrow, ne), F32)),
        grid=(nrow // tm,),
        in_specs=[pl.BlockSpec((tm, d), lambda i: (i + off, 0)),
                  pl.BlockSpec((1, 6, d), lambda i: (_mod_row(i + off, tm, layout), 0, 0)),
                  pl.BlockSpec((1, d), lambda i: (0, 0)),
                  pl.BlockSpec((d, ne), lambda i: (0, 0))],
        out_specs=(pl.BlockSpec((tm, d), lambda i: (i, 0)),
                   pl.BlockSpec((tm, ne), lambda i: (i, 0))),
        compiler_params=_cparams(("parallel",)),
        name="moe_router",
    )(tok, mod, g, wr_pad)


def _experts_kernel(blk_e_ref, nb_ref, x_ref, w1_ref, w3_ref, w2_ref, y_ref, acc_scr):
    g = pl.program_id(0)
    f = pl.program_id(1)
    last = pl.num_programs(1) - 1
    used = g < nb_ref[0]

    @pl.when(used)
    def _():
        @pl.when(f == 0)
        def _():
            acc_scr[...] = jnp.zeros_like(acc_scr)

        x = x_ref[...]
        a = jnp.dot(x, w1_ref[0], preferred_element_type=F32)
        b = jnp.dot(x, w3_ref[0], preferred_element_type=F32)
        u = (a * jax.nn.sigmoid(a) * b).astype(BF16)
        acc_scr[...] += jnp.dot(u, w2_ref[0], preferred_element_type=F32)

        @pl.when(f == last)
        def _():
            y_ref[...] = acc_scr[...]

    @pl.when(jnp.logical_not(used) & (f == last))
    def _():
        y_ref[...] = jnp.zeros_like(y_ref)


def _experts(buf, blk_e, nb_used, w1, w3, w2, tf=512):
    rows, d = buf.shape
    ff = w1.shape[2]
    nf = ff // tf

    def gidx(g, nb):
        return jnp.minimum(g, nb[0] - 1)

    def fidx(g, f, nb):
        return jnp.where(g < nb[0], f, nf - 1)

    grid_spec = pltpu.PrefetchScalarGridSpec(
        num_scalar_prefetch=2,
        grid=(rows // MOE_ROWS, nf),
        in_specs=[pl.BlockSpec((MOE_ROWS, d), lambda g, f, be, nb: (gidx(g, nb), 0)),
                  pl.BlockSpec((1, d, tf), lambda g, f, be, nb: (be[gidx(g, nb)], 0, fidx(g, f, nb))),
                  pl.BlockSpec((1, d, tf), lambda g, f, be, nb: (be[gidx(g, nb)], 0, fidx(g, f, nb))),
                  pl.BlockSpec((1, tf, d), lambda g, f, be, nb: (be[gidx(g, nb)], fidx(g, f, nb), 0))],
        out_specs=pl.BlockSpec((MOE_ROWS, d), lambda g, f, be, nb: (g, 0)),
        scratch_shapes=[pltpu.VMEM((MOE_ROWS, d), F32)],
    )
    return pl.pallas_call(
        _experts_kernel,
        out_shape=jax.ShapeDtypeStruct((rows, d), F32),
        grid_spec=grid_spec,
        compiler_params=_cparams(("arbitrary", "arbitrary")),
        name="moe_experts",
    )(blk_e, nb_used, buf, w1, w3, w2)


def _combine_kernel(x_ref, mod_ref, y0_ref, y1_ref, w_ref, gf_ref, o_ref):
    w = w_ref[...]
    y = w[:, 0:1] * y0_ref[...] + w[:, 1:2] * y1_ref[...]
    x = x_ref[...] + mod_ref[0, 5:6, :] * y
    o_ref[...] = x * lax.rsqrt(jnp.mean(x * x, axis=-1, keepdims=True) + EPS) * gf_ref[...]


def _combine_final(tok, mod, y0, y1, top_w, g_final, layout, skip_rows):
    rows, d = tok.shape
    tm = ROW_TILE
    off = skip_rows // tm
    nrow = rows - skip_rows
    return pl.pallas_call(
        _combine_kernel,
        out_shape=jax.ShapeDtypeStruct((nrow, d), F32),
        grid=(nrow // tm,),
        in_specs=[pl.BlockSpec((tm, d), lambda i: (i + off, 0)),
                  pl.BlockSpec((1, 6, d), lambda i: (_mod_row(i + off, tm, layout), 0, 0)),
                  pl.BlockSpec((tm, d), lambda i: (i, 0)),
                  pl.BlockSpec((tm, d), lambda i: (i, 0)),
                  pl.BlockSpec((tm, TOP_K), lambda i: (i, 0)),
                  pl.BlockSpec((1, d), lambda i: (0, 0))],
        out_specs=pl.BlockSpec((tm, d), lambda i: (i, 0)),
        compiler_params=_cparams(("parallel",)),
        name="moe_combine_final",
    )(tok, mod, y0, y1, top_w, g_final.reshape(1, d))


def _route(logits):
    n = logits.shape[0]
    top_logit, top_e = lax.top_k(logits, TOP_K)
    top_w = jax.nn.softmax(top_logit, axis=-1)
    a_e = top_e.reshape(-1).astype(jnp.int32)
    order = jnp.argsort(a_e, stable=True).astype(jnp.int32)
    counts = jnp.bincount(a_e, length=N_EXPERTS).astype(jnp.int32)
    padded = (counts + MOE_ROWS - 1) // MOE_ROWS * MOE_ROWS
    pad_end = jnp.cumsum(padded)
    pad_start = pad_end - padded
    start = jnp.cumsum(counts) - counts
    e_s = a_e[order]
    dest_s = pad_start[e_s] + (jnp.arange(n * TOP_K, dtype=jnp.int32) - start[e_s])
    n_blocks = n * TOP_K // MOE_ROWS + N_EXPERTS
    slot_tok = jnp.zeros((n_blocks * MOE_ROWS,), jnp.int32).at[dest_s].set(order // TOP_K)
    dest = jnp.zeros((n * TOP_K,), jnp.int32).at[order].set(dest_s).reshape(n, TOP_K)
    blk_e = jnp.minimum(jnp.searchsorted(pad_end, jnp.arange(n_blocks, dtype=jnp.int32) * MOE_ROWS,
                                         side='right'), N_EXPERTS - 1).astype(jnp.int32)
    nb_used = (pad_end[-1:] // MOE_ROWS).astype(jnp.int32)
    return top_w, slot_tok, dest, blk_e, nb_used


def kernel(x, c, ctx, c_ctx, w_ada, b_ada, g_mix, w_in, conv_w, conv_b, conv_ln_g, conv_ln_b, hgrn_lb_logits, hgrn_norm_g, w_out, g_ffn, ffn_w1, ffn_w3, ffn_w2, router_w, moe_w1, moe_w3, moe_w2, g_final):
    batch, seq, d = x.shape
    ctx_len = ctx.shape[1]
    depth = w_ada.shape[0]
    assert depth == 2 and batch == 2
    conv_ch = conv_w.shape[2]
    ctx_rows = batch * ctx_len
    layout = (ctx_rows, seq)

    cs = jnp.cumsum(jax.nn.softmax(hgrn_lb_logits.astype(F32), axis=1), axis=1)
    lb_all = cs - cs[:, :1]

    cc = jnp.zeros((MOD_ROWS, d), F32).at[0:batch].set(c).at[batch].set(c_ctx)
    mod_all = _adaln(cc, w_ada, b_ada).reshape(depth, MOD_ROWS, 6, d)

    tok = jnp.concatenate([ctx.reshape(ctx_rows, d), x.reshape(batch * seq, d)], axis=0)
    row = lambda a: a.reshape(1, -1)

    for l in range(depth):
        last = l == depth - 1
        mod = mod_all[l]
        px = _in_proj(tok, mod, row(g_mix[l]), w_in[l].astype(BF16), layout)
        o = _hgrn_scan(px, lb_all[:, l], batch, ctx_len, seq, 2 * conv_ch)
        conv = _conformer_conv(px, conv_w[l], conv_b[l], conv_ln_g[l], conv_ln_b[l], ctx_len,
                               with_ctx=not last, skip_blocks=ctx_rows // ctx_len)
        skip_rows = ctx_rows if last else 0
        tok = _out_proj(tok, mod, conv, o, px, hgrn_norm_g[l], w_out[l].astype(BF16), layout, skip_rows)
        j = l // 2
        if l % 2 == 0:
            tok = _dense_ffn(tok, mod, row(g_ffn[l]), ffn_w1[j].astype(BF16), ffn_w3[j].astype(BF16),
                             ffn_w2[j].astype(BF16), layout)
        else:
            assert last
            wr_pad = jnp.zeros((d, 128), F32).at[:, :N_EXPERTS].set(router_w[j])
            h, logits = _router(tok, mod, row(g_ffn[l]), wr_pad, layout, skip_rows)
            top_w, slot_tok, dest, blk_e, nb_used = _route(logits[:, :N_EXPERTS])
            buf = jnp.take(h, slot_tok, axis=0)
            y = _experts(buf, blk_e, nb_used, moe_w1[j].astype(BF16), moe_w3[j].astype(BF16),
                         moe_w2[j].astype(BF16))
            y0 = jnp.take(y, dest[:, 0], axis=0)
            y1 = jnp.take(y, dest[:, 1], axis=0)
            out = _combine_final(tok, mod, y0, y1, top_w, g_final, layout, skip_rows)
    return out.reshape(batch, seq, d)
```

```python
import functools
import math

import numpy as np
import jax
import jax.numpy as jnp
from jax import lax
from jax.experimental import pallas as pl
from jax.experimental.pallas import tpu as pltpu

F32 = jnp.float32
BF16 = jnp.bfloat16
I32 = jnp.int32
EPS = 1e-6
LOG2E = math.log2(math.e)

SUBLANES = 8
GRID_W = 64
CONV_K = 31
CONV_PAD = 16
HG_HEADS = 8
HG_D = 128
HG_CHUNK = 128
N_EXPERTS = 8
TOP_K = 2
MOE_SUB = 512
MOE_ROWS = 2 * MOE_SUB
ROW_TILE = 512
OUT_ROW_TILE = 256
MOD_ROWS = 8
DMA_UNROLL = 8
V7X_VMEM_LIMIT = 56 * 1024 * 1024

_NN = (((1,), (0,)), ((), ()))
_NT = (((1,), (1,)), ((), ()))
_TN = (((0,), (0,)), ((), ()))


def _cparams(sem):
    return pltpu.CompilerParams(dimension_semantics=sem, vmem_limit_bytes=V7X_VMEM_LIMIT)


def _mod_row(i, tm, layout):
    ctx_rows, seq = layout
    assert ctx_rows % tm == 0 and seq % tm == 0
    return jnp.where(i < ctx_rows // tm, 2, (i - ctx_rows // tm) // (seq // tm))


def _norm_mod(x, g, shift, scale):
    y = x * lax.rsqrt(jnp.mean(x * x, axis=-1, keepdims=True) + EPS) * g
    return y * (1.0 + scale) + shift


def _ada_kernel(c_ref, w_ref, b_ref, o_ref):
    c = c_ref[...]
    a = c * jax.nn.sigmoid(c)
    o_ref[0] = jnp.dot(a, w_ref[0], preferred_element_type=F32,
                       precision=lax.Precision.HIGHEST) + b_ref[0]


def _adaln(cc, w_ada, b_ada, tn=1024):
    depth, d, n = w_ada.shape
    return pl.pallas_call(
        _ada_kernel,
        out_shape=jax.ShapeDtypeStruct((depth, MOD_ROWS, n), F32),
        grid=(depth, n // tn),
        in_specs=[pl.BlockSpec((MOD_ROWS, d), lambda l, j: (0, 0)),
                  pl.BlockSpec((1, d, tn), lambda l, j: (l, 0, j)),
                  pl.BlockSpec((1, 1, tn), lambda l, j: (l, 0, j))],
        out_specs=pl.BlockSpec((1, MOD_ROWS, tn), lambda l, j: (l, 0, j)),
        compiler_params=_cparams(("parallel", "parallel")),
        name="adaln",
    )(cc, w_ada, b_ada.reshape(depth, 1, n))


def _proj_kernel(x_ref, mod_ref, g_ref, w_ref, o_ref, h_scr):
    @pl.when(pl.program_id(1) == 0)
    def _():
        h = _norm_mod(x_ref[...], g_ref[...], mod_ref[0, 0:1, :], mod_ref[0, 1:2, :])
        h_scr[...] = h.astype(BF16)

    o_ref[...] = jnp.dot(h_scr[...], w_ref[...], preferred_element_type=F32)


def _in_proj(tok, mod, g, w, layout, tn=1024):
    rows, d = tok.shape
    n = w.shape[1]
    tm = ROW_TILE
    return pl.pallas_call(
        _proj_kernel,
        out_shape=jax.ShapeDtypeStruct((rows, n), F32),
        grid=(rows // tm, n // tn),
        in_specs=[pl.BlockSpec((tm, d), lambda i, j: (i, 0)),
                  pl.BlockSpec((1, 6, d), lambda i, j: (_mod_row(i, tm, layout), 0, 0)),
                  pl.BlockSpec((1, d), lambda i, j: (0, 0)),
                  pl.BlockSpec((d, tn), lambda i, j: (0, j))],
        out_specs=pl.BlockSpec((tm, tn), lambda i, j: (i, j)),
        scratch_shapes=[pltpu.VMEM((tm, d), BF16)],
        compiler_params=_cparams(("parallel", "arbitrary")),
        name="in_proj",
    )(tok, mod, g, w)


def _conv_segment(u, w_ref, pad_scr, shift_scr, seg_len):
    ch = u.shape[1]
    zeros = jnp.zeros((CONV_PAD, ch), F32)
    pad_scr[0:CONV_PAD, :] = zeros
    pad_scr[CONV_PAD + seg_len:2 * CONV_PAD + seg_len, :] = zeros
    pad_scr[CONV_PAD:CONV_PAD + seg_len, :] = u
    base = CONV_PAD - CONV_K // 2
    span = seg_len + (CONV_K - 1) // SUBLANES * SUBLANES
    acc = jnp.zeros((seg_len, ch), F32)
    for r in range(SUBLANES):
        shift_scr[0:span, :] = pad_scr[base + r:base + r + span, :]
        for k in range(r, CONV_K, SUBLANES):
            acc = acc + shift_scr[k - r:k - r + seg_len, :] * w_ref[k:k + 1, :]
    return acc


def _conv_kernel(val_ref, gate_ref, w_ref, b_ref, lg_ref, lb_ref, o_ref, pad_scr, shift_scr, *,
                 ctx_blocks, ctx_len):
    rows = val_ref.shape[0]

    def run(seg_len):
        for s in range(rows // seg_len):
            sl = slice(s * seg_len, (s + 1) * seg_len)
            u = val_ref[sl, :] * jax.nn.sigmoid(gate_ref[sl, :])
            y = _conv_segment(u, w_ref, pad_scr, shift_scr, seg_len) + b_ref[...]
            mu = jnp.mean(y, axis=-1, keepdims=True)
            yc = y - mu
            var = jnp.mean(yc * yc, axis=-1, keepdims=True)
            z = yc * lax.rsqrt(var + EPS) * lg_ref[...] + lb_ref[...]
            o_ref[sl, :] = (z * jax.nn.sigmoid(z)).astype(o_ref.dtype)

    if ctx_blocks:
        is_ctx = pl.program_id(0) < ctx_blocks
        pl.when(is_ctx)(lambda: run(ctx_len))
        pl.when(jnp.logical_not(is_ctx))(lambda: run(GRID_W))
    else:
        run(GRID_W)


def _conformer_conv(px, w, b, lg, lb, ctx_len, with_ctx, skip_blocks):
    rows = px.shape[0]
    ch = w.shape[1]
    tm = ctx_len
    off = 0 if with_ctx else skip_blocks
    kern = functools.partial(_conv_kernel, ctx_blocks=skip_blocks if with_ctx else 0, ctx_len=ctx_len)
    vec = lambda a: a.reshape(1, ch)
    return pl.pallas_call(
        kern,
        out_shape=jax.ShapeDtypeStruct((rows - off * tm, ch), BF16),
        grid=(rows // tm - off,),
        in_specs=[pl.BlockSpec((tm, ch), lambda i: (i + off, 0)),
                  pl.BlockSpec((tm, ch), lambda i: (i + off, 1)),
                  pl.BlockSpec((CONV_K, ch), lambda i: (0, 0)),
                  pl.BlockSpec((1, ch), lambda i: (0, 0)),
                  pl.BlockSpec((1, ch), lambda i: (0, 0)),
                  pl.BlockSpec((1, ch), lambda i: (0, 0))],
        out_specs=pl.BlockSpec((tm, ch), lambda i: (i, 0)),
        scratch_shapes=[pltpu.VMEM((tm + 2 * CONV_PAD, ch), F32),
                        pltpu.VMEM((tm + 2 * CONV_PAD, ch), F32)],
        compiler_params=_cparams(("parallel",)),
        name="conformer_conv",
    )(px, px, w, vec(b), vec(lg), vec(lb))


def _hgrn_levels(c):
    return [c >> (i + 1) for i in range(c.bit_length() - 1)]


def _hgrn_constants(c, rev):
    p = np.arange(c)[::-1] if rev else np.arange(c)
    pt, ps = p[:, None], p[None, :]
    tri = (ps <= pt).astype(np.float32)
    masks = [((pt // (2 * m)) == (ps // (2 * m))) & ((pt // m) % 2 == 1) & ((ps // m) % 2 == 0)
             for m in _hgrn_levels(c)]
    return jnp.asarray(tri), jnp.asarray(np.stack(masks).astype(np.float32))


def _hgrn_ref_delta(bh, b_scr, sl, m, rev):
    c = bh.shape[0]
    off = m if rev else m - 1
    pieces = []
    if 2 * m >= SUBLANES:
        for j in range(c // (2 * m)):
            r = j * 2 * m + off
            pieces.append(bh[j * 2 * m:(j + 1) * 2 * m, :] - b_scr[r:r + 1, sl])
    else:
        assert 4 * m == SUBLANES
        upper = lax.broadcasted_iota(I32, (SUBLANES, bh.shape[1]), 0) < 2 * m
        for g in range(c // SUBLANES):
            r = g * SUBLANES + off
            ref = jnp.where(upper, b_scr[r:r + 1, sl], b_scr[r + 2 * m:r + 2 * m + 1, sl])
            pieces.append(bh[g * SUBLANES:(g + 1) * SUBLANES, :] - ref)
    return jnp.concatenate(pieces, axis=0)


def _hgrn_kernel(q_ref, z_ref, v_ref, lb_ref, tri_ref, mask_ref, o_ref, st_scr, f_scr, b_scr, *, rev):
    c = q_ref.shape[0]

    @pl.when(pl.program_id(1) == 0)
    def _():
        st_scr[...] = jnp.zeros_like(st_scr)

    lb = lb_ref[...]
    f = lb + (1.0 - lb) * jax.nn.sigmoid(z_ref[...])
    f_scr[...] = f
    b_scr[...] = jnp.dot(tri_ref[...], jnp.log(f) * LOG2E, preferred_element_type=F32,
                         precision=lax.Precision.HIGHEST)
    last = 0 if rev else c - 1
    levels = _hgrn_levels(c)

    for h in range(q_ref.shape[1] // HG_D):
        sl = slice(h * HG_D, (h + 1) * HG_D)
        qh = q_ref[:, sl]
        vh = v_ref[:, sl]
        fh = f_scr[:, sl]
        kh = 1.0 - fh
        bh = b_scr[:, sl]
        b_last = b_scr[last:last + 1, sl]
        st = st_scr[h]
        vb = vh.astype(BF16)

        o = lax.dot_general((qh * jnp.exp2(bh)).astype(BF16), st.astype(BF16), _NT,
                            preferred_element_type=F32)
        scores = jnp.zeros((c, c), F32)
        for li, m in enumerate(levels):
            if m == 1:
                qm, km = qh * fh, kh
            else:
                dlt = _hgrn_ref_delta(bh, b_scr, sl, m, rev)
                e = jnp.exp2(-jnp.abs(dlt))
                qm, km = qh * e, kh * e
            sc = lax.dot_general(qm.astype(BF16), km.astype(BF16), _NT, preferred_element_type=F32)
            scores = scores + sc * mask_ref[li]
        o = o + jnp.dot(scores.astype(BF16), vb, preferred_element_type=F32)
        o = o + jnp.sum(qh * kh, axis=-1, keepdims=True) * vh
        o_ref[:, sl] = o

        ke = (kh * jnp.exp2(b_last - bh)).astype(BF16)
        st_scr[h] = st * jnp.exp2(b_last) + lax.dot_general(vb, ke, _TN, preferred_element_type=F32)


def _hgrn_scan(px, lb, rev, batch, ctx_len, seq_len, col0):
    rows = px.shape[0]
    c = HG_CHUNK
    w = HG_HEADS * HG_D
    cb = col0 // w
    ncx, nl = ctx_len // c, seq_len // c
    lat0 = batch * ncx

    def row_block(b, n):
        nc = ncx - 1 - n if rev else n
        nx = nl - 1 - (n - ncx) if rev else n - ncx
        return jnp.where(n < ncx, b * ncx + nc, lat0 + b * nl + nx)

    tri, masks = _hgrn_constants(c, rev)
    zcol = cb + (2 if rev else 1)
    return pl.pallas_call(
        functools.partial(_hgrn_kernel, rev=rev),
        out_shape=jax.ShapeDtypeStruct((rows, w), F32),
        grid=(batch, ncx + nl),
        in_specs=[pl.BlockSpec((c, w), lambda b, n: (row_block(b, n), cb)),
                  pl.BlockSpec((c, w), lambda b, n: (row_block(b, n), zcol)),
                  pl.BlockSpec((c, w), lambda b, n: (row_block(b, n), cb + 3)),
                  pl.BlockSpec((1, w), lambda b, n: (0, 0)),
                  pl.BlockSpec((c, c), lambda b, n: (0, 0)),
                  pl.BlockSpec(masks.shape, lambda b, n: (0, 0, 0))],
        out_specs=pl.BlockSpec((c, w), lambda b, n: (row_block(b, n), 0)),
        scratch_shapes=[pltpu.VMEM((HG_HEADS, HG_D, HG_D), F32),
                        pltpu.VMEM((c, w), F32),
                        pltpu.VMEM((c, w), F32)],
        compiler_params=_cparams(("parallel", "arbitrary")),
        name="hgrn_scan_bwd" if rev else "hgrn_scan_fwd",
    )(px, px, px, lb.reshape(1, w), tri, masks)


def _out_kernel(x_ref, mod_ref, conv_ref, of_ref, ob_ref, g_ref, ng_ref, w_ref, y_ref):
    o = of_ref[...] + ob_ref[...]
    parts = []
    for h in range(HG_HEADS):
        oh = o[:, h * HG_D:(h + 1) * HG_D]
        parts.append(oh * lax.rsqrt(jnp.mean(oh * oh, axis=-1, keepdims=True) + EPS))
    g = g_ref[...]
    rec = jnp.concatenate(parts, axis=-1) * ng_ref[...] * (g * jax.nn.sigmoid(g))
    ch = conv_ref.shape[1]
    acc = jnp.dot(conv_ref[...], w_ref[0:ch, :], preferred_element_type=F32)
    acc = acc + jnp.dot(rec.astype(BF16), w_ref[ch:, :], preferred_element_type=F32)
    y_ref[...] = x_ref[...] + mod_ref[0, 2:3, :] * acc


def _out_proj(tok, mod, conv, o_f, o_b, px, norm_g, w, layout, skip_rows):
    rows, d = tok.shape
    ch = conv.shape[1]
    wr = o_f.shape[1]
    gcol = px.shape[1] // wr - 1
    tm = OUT_ROW_TILE
    off = skip_rows // tm
    assert conv.shape[0] == rows - skip_rows
    return pl.pallas_call(
        _out_kernel,
        out_shape=jax.ShapeDtypeStruct((rows - skip_rows, d), F32),
        grid=(rows // tm - off,),
        in_specs=[pl.BlockSpec((tm, d), lambda i: (i + off, 0)),
                  pl.BlockSpec((1, 6, d), lambda i: (_mod_row(i + off, tm, layout), 0, 0)),
                  pl.BlockSpec((tm, ch), lambda i: (i, 0)),
                  pl.BlockSpec((tm, wr), lambda i: (i + off, 0)),
                  pl.BlockSpec((tm, wr), lambda i: (i + off, 0)),
                  pl.BlockSpec((tm, wr), lambda i: (i + off, gcol)),
                  pl.BlockSpec((1, wr), lambda i: (0, 0)),
                  pl.BlockSpec((ch + wr, d), lambda i: (0, 0))],
        out_specs=pl.BlockSpec((tm, d), lambda i: (i, 0)),
        compiler_params=_cparams(("parallel",)),
        name="out_proj",
    )(tok, mod, conv, o_f, o_b, px, norm_g.reshape(1, wr), w)


def _ffn_kernel(x_ref, mod_ref, g_ref, w1_ref, w3_ref, w2_ref, y_ref, h_scr, acc_scr):
    f = pl.program_id(1)

    @pl.when(f == 0)
    def _():
        h = _norm_mod(x_ref[...], g_ref[...], mod_ref[0, 3:4, :], mod_ref[0, 4:5, :])
        h_scr[...] = h.astype(BF16)
        acc_scr[...] = jnp.zeros_like(acc_scr)

    h = h_scr[...]
    a = jnp.dot(h, w1_ref[...], preferred_element_type=F32)
    b = jnp.dot(h, w3_ref[...], preferred_element_type=F32)
    u = (a * jax.nn.sigmoid(a) * b).astype(BF16)
    acc_scr[...] += jnp.dot(u, w2_ref[...], preferred_element_type=F32)

    @pl.when(f == pl.num_programs(1) - 1)
    def _():
        y_ref[...] = x_ref[...] + mod_ref[0, 5:6, :] * acc_scr[...]


def _dense_ffn(tok, mod, g, w1, w3, w2, layout, tf=512):
    rows, d = tok.shape
    ff = w1.shape[1]
    tm = ROW_TILE
    return pl.pallas_call(
        _ffn_kernel,
        out_shape=jax.ShapeDtypeStruct((rows, d), F32),
        grid=(rows // tm, ff // tf),
        in_specs=[pl.BlockSpec((tm, d), lambda i, f: (i, 0)),
                  pl.BlockSpec((1, 6, d), lambda i, f: (_mod_row(i, tm, layout), 0, 0)),
                  pl.BlockSpec((1, d), lambda i, f: (0, 0)),
                  pl.BlockSpec((d, tf), lambda i, f: (0, f)),
                  pl.BlockSpec((d, tf), lambda i, f: (0, f)),
                  pl.BlockSpec((tf, d), lambda i, f: (f, 0))],
        out_specs=pl.BlockSpec((tm, d), lambda i, f: (i, 0)),
        scratch_shapes=[pltpu.VMEM((tm, d), BF16), pltpu.VMEM((tm, d), F32)],
        compiler_params=_cparams(("parallel", "arbitrary")),
        name="dense_ffn",
    )(tok, mod, g, w1, w3, w2)


def _router_kernel(x_ref, mod_ref, g_ref, wr_ref, h_ref, lg_ref):
    h = _norm_mod(x_ref[...], g_ref[...], mod_ref[0, 3:4, :], mod_ref[0, 4:5, :])
    h_ref[...] = h
    lg_ref[...] = jnp.dot(h, wr_ref[...], preferred_element_type=F32,
                          precision=lax.Precision.HIGHEST)


def _router(tok, mod, g, wr_pad, layout, skip_rows):
    rows, d = tok.shape
    tm = ROW_TILE
    off = skip_rows // tm
    nrow = rows - skip_rows
    ne = wr_pad.shape[1]
    return pl.pallas_call(
        _router_kernel,
        out_shape=(jax.ShapeDtypeStruct((nrow, d), F32), jax.ShapeDtypeStruct((nrow, ne), F32)),
        grid=(nrow // tm,),
        in_specs=[pl.BlockSpec((tm, d), lambda i: (i + off, 0)),
                  pl.BlockSpec((1, 6, d), lambda i: (_mod_row(i + off, tm, layout), 0, 0)),
                  pl.BlockSpec((1, d), lambda i: (0, 0)),
                  pl.BlockSpec((d, ne), lambda i: (0, 0))],
        out_specs=(pl.BlockSpec((tm, d), lambda i: (i, 0)),
                   pl.BlockSpec((tm, ne), lambda i: (i, 0))),
        compiler_params=_cparams(("parallel",)),
        name="moe_router",
    )(tok, mod, g, wr_pad)


def _row_copy(src_hbm, src_row, dst_vmem, dst_row, sem):
    return pltpu.make_async_copy(src_hbm.at[pl.ds(src_row, 1), :], dst_vmem.at[pl.ds(dst_row, 1), :], sem)


def _gather_rows_dma(idx_ref, idx_base, idx_stride, src_hbm, dst_vmem, nrows, sem):
    assert nrows % DMA_UNROLL == 0

    def issue(i, carry):
        for u in range(DMA_UNROLL):
            r = i * DMA_UNROLL + u
            _row_copy(src_hbm, idx_ref[idx_base + r * idx_stride], dst_vmem, r, sem).start()
        return carry

    lax.fori_loop(0, nrows // DMA_UNROLL, issue, 0)
    pltpu.make_async_copy(src_hbm.at[pl.ds(0, nrows), :], dst_vmem.at[pl.ds(0, nrows), :], sem).wait()


def _experts_kernel(blk_e_ref, nsub_ref, nb_ref, slot_ref, h_hbm, w1_ref, w3_ref, w2_ref, y_ref,
                    xf_scr, xb_scr, sem):
    g = pl.program_id(0)
    f = pl.program_id(1)
    used = g < nb_ref[0]

    @pl.when((f == 0) & jnp.logical_not(used))
    def _():
        y_ref[...] = jnp.zeros_like(y_ref)

    @pl.when(used)
    def _():
        for nsub in range(1, MOE_ROWS // MOE_SUB + 1):
            nrows = nsub * MOE_SUB

            @pl.when(nsub_ref[g] == nsub)
            def _():
                @pl.when(f == 0)
                def _():
                    _gather_rows_dma(slot_ref, g * MOE_ROWS, 1, h_hbm, xf_scr, nrows, sem)
                    xb_scr[0:nrows, :] = xf_scr[0:nrows, :].astype(BF16)
                    y_ref[...] = jnp.zeros_like(y_ref)

                x = xb_scr[0:nrows, :]
                a = lax.dot_general(x, w1_ref[0], _NN, preferred_element_type=F32)
                b = lax.dot_general(x, w3_ref[0], _NN, preferred_element_type=F32)
                u = (a * jax.nn.sigmoid(a) * b).astype(BF16)
                y_ref[0:nrows, :] += lax.dot_general(u, w2_ref[0], _NN, preferred_element_type=F32)


def _experts(h, slot_tok, blk_e, nsub, nb_used, w1, w3, w2, tf=256):
    n, d = h.shape
    ff = w1.shape[2]
    nf = ff // tf
    n_blocks = slot_tok.shape[0] // MOE_ROWS

    def gidx(g, nb):
        return jnp.minimum(g, nb[0] - 1)

    def fidx(g, f, nb):
        return jnp.where(g < nb[0], f, nf - 1)

    grid_spec = pltpu.PrefetchScalarGridSpec(
        num_scalar_prefetch=4,
        grid=(n_blocks, nf),
        in_specs=[pl.BlockSpec(memory_space=pl.ANY),
                  pl.BlockSpec((1, d, tf), lambda g, f, be, ns, nb, st: (be[gidx(g, nb)], 0, fidx(g, f, nb))),
                  pl.BlockSpec((1, d, tf), lambda g, f, be, ns, nb, st: (be[gidx(g, nb)], 0, fidx(g, f, nb))),
                  pl.BlockSpec((1, tf, d), lambda g, f, be, ns, nb, st: (be[gidx(g, nb)], fidx(g, f, nb), 0))],
        out_specs=pl.BlockSpec((MOE_ROWS, d), lambda g, f, be, ns, nb, st: (g, 0)),
        scratch_shapes=[pltpu.VMEM((MOE_ROWS, d), F32),
                        pltpu.VMEM((MOE_ROWS, d), BF16),
                        pltpu.SemaphoreType.DMA],
    )
    return pl.pallas_call(
        _experts_kernel,
        out_shape=jax.ShapeDtypeStruct((n_blocks * MOE_ROWS, d), F32),
        grid_spec=grid_spec,
        compiler_params=_cparams(("arbitrary", "arbitrary")),
        name="moe_experts",
    )(blk_e, nsub, nb_used, slot_tok, h, w1, w3, w2)


def _combine_kernel(dest_ref, x_ref, mod_ref, w_ref, gf_ref, y_hbm, o_ref, y0_scr, y1_scr, sem):
    tm = x_ref.shape[0]
    base = pl.program_id(0) * tm * TOP_K
    _gather_rows_dma(dest_ref, base, TOP_K, y_hbm, y0_scr, tm, sem.at[0])
    _gather_rows_dma(dest_ref, base + 1, TOP_K, y_hbm, y1_scr, tm, sem.at[1])
    w = w_ref[...]
    y = w[:, 0:1] * y0_scr[...] + w[:, 1:2] * y1_scr[...]
    x = x_ref[...] + mod_ref[0, 5:6, :] * y
    o_ref[...] = x * lax.rsqrt(jnp.mean(x * x, axis=-1, keepdims=True) + EPS) * gf_ref[...]


def _combine_final(tok, mod, y, dest, top_w, g_final, layout, skip_rows):
    rows, d = tok.shape
    tm = ROW_TILE
    off = skip_rows // tm
    nrow = rows - skip_rows
    grid_spec = pltpu.PrefetchScalarGridSpec(
        num_scalar_prefetch=1,
        grid=(nrow // tm,),
        in_specs=[pl.BlockSpec((tm, d), lambda i, ds: (i + off, 0)),
                  pl.BlockSpec((1, 6, d), lambda i, ds: (_mod_row(i + off, tm, layout), 0, 0)),
                  pl.BlockSpec((tm, TOP_K), lambda i, ds: (i, 0)),
                  pl.BlockSpec((1, d), lambda i, ds: (0, 0)),
                  pl.BlockSpec(memory_space=pl.ANY)],
        out_specs=pl.BlockSpec((tm, d), lambda i, ds: (i, 0)),
        scratch_shapes=[pltpu.VMEM((tm, d), F32), pltpu.VMEM((tm, d), F32),
                        pltpu.SemaphoreType.DMA((TOP_K,))],
    )
    return pl.pallas_call(
        _combine_kernel,
        out_shape=jax.ShapeDtypeStruct((nrow, d), F32),
        grid_spec=grid_spec,
        compiler_params=_cparams(("arbitrary",)),
        name="moe_combine_final",
    )(dest, tok, mod, top_w, g_final.reshape(1, d), y)


def _route(logits):
    n = logits.shape[0]
    na = n * TOP_K
    top_logit, top_e = lax.top_k(logits, TOP_K)
    top_w = jax.nn.softmax(top_logit, axis=-1)
    a_e = top_e.reshape(-1).astype(I32)
    order = jnp.argsort(a_e, stable=True).astype(I32)
    inv = jnp.argsort(order).astype(I32)
    counts = jnp.sum(a_e[:, None] == jnp.arange(N_EXPERTS, dtype=I32)[None, :], axis=0).astype(I32)
    padded = (counts + MOE_ROWS - 1) // MOE_ROWS * MOE_ROWS
    pad_end = jnp.cumsum(padded)
    pad_start = pad_end - padded
    start = jnp.cumsum(counts) - counts
    dest = inv + (pad_start - start)[a_e]

    n_blocks = na // MOE_ROWS + N_EXPERTS
    slot = jnp.arange(n_blocks * MOE_ROWS, dtype=I32)
    slot_e = jnp.minimum(jnp.sum(slot[:, None] >= pad_end[None, :], axis=1), N_EXPERTS - 1).astype(I32)
    rank = slot - pad_start[slot_e]
    src = jnp.clip(start[slot_e] + rank, 0, na - 1)
    slot_tok = jnp.where(rank < counts[slot_e], order[src] // TOP_K, slot % n)

    blk0 = jnp.arange(n_blocks, dtype=I32) * MOE_ROWS
    blk_e = slot_e[::MOE_ROWS]
    blk_rows = jnp.clip(counts[blk_e] - (blk0 - pad_start[blk_e]), 0, MOE_ROWS)
    nb_used = (pad_end[-1:] // MOE_ROWS).astype(I32)
    nsub = jnp.where(blk0 < pad_end[-1], (blk_rows + MOE_SUB - 1) // MOE_SUB, 0).astype(I32)
    return top_w, slot_tok, dest, blk_e, nsub, nb_used


def kernel(x, c, ctx, c_ctx, w_ada, b_ada, g_mix, w_in, conv_w, conv_b, conv_ln_g, conv_ln_b, hgrn_lb_logits, hgrn_norm_g, w_out, g_ffn, ffn_w1, ffn_w3, ffn_w2, router_w, moe_w1, moe_w3, moe_w2, g_final):
    batch, seq, d = x.shape
    ctx_len = ctx.shape[1]
    depth = w_ada.shape[0]
    assert depth == 2 and batch == 2
    conv_ch = conv_w.shape[2]
    ctx_rows = batch * ctx_len
    layout = (ctx_rows, seq)

    cs = jnp.cumsum(jax.nn.softmax(hgrn_lb_logits.astype(F32), axis=1), axis=1)
    lb_all = cs - cs[:, :1]

    cc = jnp.zeros((MOD_ROWS, d), F32).at[0:batch].set(c).at[batch].set(c_ctx)
    mod_all = _adaln(cc, w_ada, b_ada).reshape(depth, MOD_ROWS, 6, d)

    tok = jnp.concatenate([ctx.reshape(ctx_rows, d), x.reshape(batch * seq, d)], axis=0)
    row = lambda a: a.reshape(1, -1)

    for l in range(depth):
        last = l == depth - 1
        mod = mod_all[l]
        px = _in_proj(tok, mod, row(g_mix[l]), w_in[l].astype(BF16), layout)
        o_f = _hgrn_scan(px, lb_all[0, l], False, batch, ctx_len, seq, 2 * conv_ch)
        o_b = _hgrn_scan(px, lb_all[1, l], True, batch, ctx_len, seq, 2 * conv_ch)
        conv = _conformer_conv(px, conv_w[l], conv_b[l], conv_ln_g[l], conv_ln_b[l], ctx_len,
                               with_ctx=not last, skip_blocks=ctx_rows // ctx_len)
        skip_rows = ctx_rows if last else 0
        tok = _out_proj(tok, mod, conv, o_f, o_b, px, hgrn_norm_g[l], w_out[l].astype(BF16), layout,
                        skip_rows)
        j = l // 2
        if l % 2 == 0:
            tok = _dense_ffn(tok, mod, row(g_ffn[l]), ffn_w1[j].astype(BF16), ffn_w3[j].astype(BF16),
                             ffn_w2[j].astype(BF16), layout)
        else:
            assert last
            wr_pad = jnp.zeros((d, 128), F32).at[:, :N_EXPERTS].set(router_w[j])
            lat_layout = (0, seq)
            h, logits = _router(tok, mod, row(g_ffn[l]), wr_pad, lat_layout, 0)
            top_w, slot_tok, dest, blk_e, nsub, nb_used = _route(logits[:, :N_EXPERTS])
            y = _experts(h, slot_tok, blk_e, nsub, nb_used, moe_w1[j], moe_w3[j], moe_w2[j])
            out = _combine_final(tok, mod, y, dest, top_w, g_final, lat_layout, 0)
    return out.reshape(batch, seq, d)
```

```python
import functools
import math

import numpy as np
import jax
import jax.numpy as jnp
from jax import lax
from jax.experimental import pallas as pl
from jax.experimental.pallas import tpu as pltpu

F32 = jnp.float32
BF16 = jnp.bfloat16
I32 = jnp.int32
EPS = 1e-6
LOG2E = math.log2(math.e)

SUBLANES = 8
GRID_W = 64
CONV_K = 31
CONV_PAD = 16
HG_HEADS = 8
HG_D = 128
HG_CHUNK = 128
N_EXPERTS = 8
TOP_K = 2
MOE_SUB = 256
MOE_ROWS = 4 * MOE_SUB
ROW_TILE = 512
OUT_ROW_TILE = 256
MOD_ROWS = 8
DMA_UNROLL = 8
V7X_VMEM_LIMIT = 56 * 1024 * 1024

_NN = (((1,), (0,)), ((), ()))
_NT = (((1,), (1,)), ((), ()))
_TN = (((0,), (0,)), ((), ()))


def _cparams(sem):
    return pltpu.CompilerParams(dimension_semantics=sem, vmem_limit_bytes=V7X_VMEM_LIMIT)


def _mod_row(i, tm, layout):
    ctx_rows, seq = layout
    assert ctx_rows % tm == 0 and seq % tm == 0
    return jnp.where(i < ctx_rows // tm, 2, (i - ctx_rows // tm) // (seq // tm))


def _norm_mod(x, g, shift, scale):
    y = x * lax.rsqrt(jnp.mean(x * x, axis=-1, keepdims=True) + EPS) * g
    return y * (1.0 + scale) + shift


def _ada_kernel(c_ref, w_ref, b_ref, o_ref):
    c = c_ref[...]
    a = c * jax.nn.sigmoid(c)
    o_ref[0] = jnp.dot(a, w_ref[0], preferred_element_type=F32,
                       precision=lax.Precision.HIGHEST) + b_ref[0]


def _adaln(cc, w_ada, b_ada, tn=1024):
    depth, d, n = w_ada.shape
    return pl.pallas_call(
        _ada_kernel,
        out_shape=jax.ShapeDtypeStruct((depth, MOD_ROWS, n), F32),
        grid=(depth, n // tn),
        in_specs=[pl.BlockSpec((MOD_ROWS, d), lambda l, j: (0, 0)),
                  pl.BlockSpec((1, d, tn), lambda l, j: (l, 0, j)),
                  pl.BlockSpec((1, 1, tn), lambda l, j: (l, 0, j))],
        out_specs=pl.BlockSpec((1, MOD_ROWS, tn), lambda l, j: (l, 0, j)),
        compiler_params=_cparams(("parallel", "parallel")),
        name="adaln",
    )(cc, w_ada, b_ada.reshape(depth, 1, n))


def _proj_kernel(x_ref, mod_ref, g_ref, w_ref, o_ref, h_scr):
    @pl.when(pl.program_id(1) == 0)
    def _():
        h = _norm_mod(x_ref[...], g_ref[...], mod_ref[0, 0:1, :], mod_ref[0, 1:2, :])
        h_scr[...] = h.astype(BF16)

    o_ref[...] = jnp.dot(h_scr[...], w_ref[...], preferred_element_type=F32)


def _in_proj(tok, mod, g, w, layout, tn=1792):
    rows, d = tok.shape
    n = w.shape[1]
    tm = ROW_TILE
    return pl.pallas_call(
        _proj_kernel,
        out_shape=jax.ShapeDtypeStruct((rows, n), F32),
        grid=(rows // tm, n // tn),
        in_specs=[pl.BlockSpec((tm, d), lambda i, j: (i, 0)),
                  pl.BlockSpec((1, 6, d), lambda i, j: (_mod_row(i, tm, layout), 0, 0)),
                  pl.BlockSpec((1, d), lambda i, j: (0, 0)),
                  pl.BlockSpec((d, tn), lambda i, j: (0, j))],
        out_specs=pl.BlockSpec((tm, tn), lambda i, j: (i, j)),
        scratch_shapes=[pltpu.VMEM((tm, d), BF16)],
        compiler_params=_cparams(("parallel", "arbitrary")),
        name="in_proj",
    )(tok, mod, g, w)


def _conv_segment(u, w_ref, pad_scr, shift_scr, seg_len):
    ch = u.shape[1]
    zeros = jnp.zeros((CONV_PAD, ch), F32)
    pad_scr[0:CONV_PAD, :] = zeros
    pad_scr[CONV_PAD + seg_len:2 * CONV_PAD + seg_len, :] = zeros
    pad_scr[CONV_PAD:CONV_PAD + seg_len, :] = u
    base = CONV_PAD - CONV_K // 2
    span = seg_len + (CONV_K - 1) // SUBLANES * SUBLANES
    acc = jnp.zeros((seg_len, ch), F32)
    for r in range(SUBLANES):
        shift_scr[0:span, :] = pad_scr[base + r:base + r + span, :]
        for k in range(r, CONV_K, SUBLANES):
            acc = acc + shift_scr[k - r:k - r + seg_len, :] * w_ref[k:k + 1, :]
    return acc


def _conv_kernel(val_ref, gate_ref, w_ref, b_ref, lg_ref, lb_ref, o_ref, pad_scr, shift_scr, *,
                 ctx_blocks, ctx_len):
    rows = val_ref.shape[0]

    def run(seg_len):
        for s in range(rows // seg_len):
            sl = slice(s * seg_len, (s + 1) * seg_len)
            u = val_ref[sl, :] * jax.nn.sigmoid(gate_ref[sl, :])
            y = _conv_segment(u, w_ref, pad_scr, shift_scr, seg_len) + b_ref[...]
            mu = jnp.mean(y, axis=-1, keepdims=True)
            yc = y - mu
            var = jnp.mean(yc * yc, axis=-1, keepdims=True)
            z = yc * lax.rsqrt(var + EPS) * lg_ref[...] + lb_ref[...]
            o_ref[sl, :] = (z * jax.nn.sigmoid(z)).astype(o_ref.dtype)

    if ctx_blocks:
        is_ctx = pl.program_id(0) < ctx_blocks
        pl.when(is_ctx)(lambda: run(ctx_len))
        pl.when(jnp.logical_not(is_ctx))(lambda: run(GRID_W))
    else:
        run(GRID_W)


def _conformer_conv(px, w, b, lg, lb, ctx_len, with_ctx, skip_blocks):
    rows = px.shape[0]
    ch = w.shape[1]
    tm = ctx_len
    off = 0 if with_ctx else skip_blocks
    kern = functools.partial(_conv_kernel, ctx_blocks=skip_blocks if with_ctx else 0, ctx_len=ctx_len)
    vec = lambda a: a.reshape(1, ch)
    return pl.pallas_call(
        kern,
        out_shape=jax.ShapeDtypeStruct((rows - off * tm, ch), BF16),
        grid=(rows // tm - off,),
        in_specs=[pl.BlockSpec((tm, ch), lambda i: (i + off, 0)),
                  pl.BlockSpec((tm, ch), lambda i: (i + off, 1)),
                  pl.BlockSpec((CONV_K, ch), lambda i: (0, 0)),
                  pl.BlockSpec((1, ch), lambda i: (0, 0)),
                  pl.BlockSpec((1, ch), lambda i: (0, 0)),
                  pl.BlockSpec((1, ch), lambda i: (0, 0))],
        out_specs=pl.BlockSpec((tm, ch), lambda i: (i, 0)),
        scratch_shapes=[pltpu.VMEM((tm + 2 * CONV_PAD, ch), F32),
                        pltpu.VMEM((tm + 2 * CONV_PAD, ch), F32)],
        compiler_params=_cparams(("parallel",)),
        name="conformer_conv",
    )(px, px, w, vec(b), vec(lg), vec(lb))


def _hgrn_levels(c):
    return [c >> (i + 1) for i in range(c.bit_length() - 1)]


def _hgrn_constants(c, rev):
    p = np.arange(c)[::-1] if rev else np.arange(c)
    pt, ps = p[:, None], p[None, :]
    tri = (ps <= pt).astype(np.float32)
    masks = [((pt // (2 * m)) == (ps // (2 * m))) & ((pt // m) % 2 == 1) & ((ps // m) % 2 == 0)
             for m in _hgrn_levels(c)]
    return jnp.asarray(tri, dtype=BF16), jnp.asarray(np.stack(masks).astype(np.float32))


def _hgrn_neg_dist(bh, b_scr, sl, m, rev):
    c = bh.shape[0]
    off = m if rev else m - 1
    pieces = []
    if m >= SUBLANES:
        for j in range(c // (2 * m)):
            lo, mid, hi = j * 2 * m, j * 2 * m + m, (j + 1) * 2 * m
            ref = b_scr[lo + off:lo + off + 1, sl]
            first, second = (bh[lo:mid, :] - ref, ref - bh[mid:hi, :]) if rev else \
                            (ref - bh[lo:mid, :], bh[mid:hi, :] - ref)
            pieces += [first, second]
    elif 2 * m == SUBLANES:
        for g in range(c // SUBLANES):
            r = g * SUBLANES + off
            pieces.append(-jnp.abs(bh[g * SUBLANES:(g + 1) * SUBLANES, :] - b_scr[r:r + 1, sl]))
    else:
        assert 4 * m == SUBLANES
        upper = lax.broadcasted_iota(I32, (SUBLANES, bh.shape[1]), 0) < 2 * m
        for g in range(c // SUBLANES):
            r = g * SUBLANES + off
            ref = jnp.where(upper, b_scr[r:r + 1, sl], b_scr[r + 2 * m:r + 2 * m + 1, sl])
            pieces.append(-jnp.abs(bh[g * SUBLANES:(g + 1) * SUBLANES, :] - ref))
    return jnp.concatenate(pieces, axis=0)


def _hgrn_kernel(q_ref, z_ref, v_ref, lb_ref, tri_ref, mask_ref, o_ref, st_scr, f_scr, b_scr, *, rev):
    c = q_ref.shape[0]

    @pl.when(pl.program_id(1) == 0)
    def _():
        st_scr[...] = jnp.zeros_like(st_scr)

    lb = lb_ref[...]
    f = lb + (1.0 - lb) * jax.nn.sigmoid(z_ref[...])
    f_scr[...] = f
    lf = jnp.log(f) * LOG2E
    lf_hi = lf.astype(BF16)
    lf_lo = (lf - lf_hi.astype(F32)).astype(BF16)
    tri = tri_ref[...]
    b_scr[...] = (jnp.dot(tri, lf_hi, preferred_element_type=F32)
                  + jnp.dot(tri, lf_lo, preferred_element_type=F32))
    last = 0 if rev else c - 1
    levels = _hgrn_levels(c)

    for h in range(q_ref.shape[1] // HG_D):
        sl = slice(h * HG_D, (h + 1) * HG_D)
        qh = q_ref[:, sl]
        vh = v_ref[:, sl]
        fh = f_scr[:, sl]
        kh = 1.0 - fh
        bh = b_scr[:, sl]
        b_last = b_scr[last:last + 1, sl]
        st = st_scr[h]
        vb = vh.astype(BF16)

        o = lax.dot_general((qh * jnp.exp2(bh)).astype(BF16), st.astype(BF16), _NT,
                            preferred_element_type=F32)
        scores = jnp.zeros((c, c), F32)
        for li, m in enumerate(levels):
            if m == 1:
                qm, km = qh * fh, kh
            else:
                e = jnp.exp2(_hgrn_neg_dist(bh, b_scr, sl, m, rev))
                qm, km = qh * e, kh * e
            sc = lax.dot_general(qm.astype(BF16), km.astype(BF16), _NT, preferred_element_type=F32)
            scores = scores + sc * mask_ref[li]
        o = o + jnp.dot(scores.astype(BF16), vb, preferred_element_type=F32)
        o = o + jnp.sum(qh * kh, axis=-1, keepdims=True) * vh
        o_ref[:, sl] = o

        ke = (kh * jnp.exp2(b_last - bh)).astype(BF16)
        st_scr[h] = st * jnp.exp2(b_last) + lax.dot_general(vb, ke, _TN, preferred_element_type=F32)


def _hgrn_scan(px, lb, rev, batch, ctx_len, seq_len, col0):
    rows = px.shape[0]
    c = HG_CHUNK
    w = HG_HEADS * HG_D
    cb = col0 // w
    ncx, nl = ctx_len // c, seq_len // c
    lat0 = batch * ncx

    def row_block(b, n):
        nc = ncx - 1 - n if rev else n
        nx = nl - 1 - (n - ncx) if rev else n - ncx
        return jnp.where(n < ncx, b * ncx + nc, lat0 + b * nl + nx)

    tri, masks = _hgrn_constants(c, rev)
    zcol = cb + (2 if rev else 1)
    return pl.pallas_call(
        functools.partial(_hgrn_kernel, rev=rev),
        out_shape=jax.ShapeDtypeStruct((rows, w), F32),
        grid=(batch, ncx + nl),
        in_specs=[pl.BlockSpec((c, w), lambda b, n: (row_block(b, n), cb)),
                  pl.BlockSpec((c, w), lambda b, n: (row_block(b, n), zcol)),
                  pl.BlockSpec((c, w), lambda b, n: (row_block(b, n), cb + 3)),
                  pl.BlockSpec((1, w), lambda b, n: (0, 0)),
                  pl.BlockSpec((c, c), lambda b, n: (0, 0)),
                  pl.BlockSpec(masks.shape, lambda b, n: (0, 0, 0))],
        out_specs=pl.BlockSpec((c, w), lambda b, n: (row_block(b, n), 0)),
        scratch_shapes=[pltpu.VMEM((HG_HEADS, HG_D, HG_D), F32),
                        pltpu.VMEM((c, w), F32),
                        pltpu.VMEM((c, w), F32)],
        compiler_params=_cparams(("parallel", "arbitrary")),
        name="hgrn_scan_bwd" if rev else "hgrn_scan_fwd",
    )(px, px, px, lb.reshape(1, w), tri, masks)


def _out_kernel(x_ref, mod_ref, conv_ref, of_ref, ob_ref, g_ref, ng_ref, w_ref, y_ref):
    o = of_ref[...] + ob_ref[...]
    parts = []
    for h in range(HG_HEADS):
        oh = o[:, h * HG_D:(h + 1) * HG_D]
        parts.append(oh * lax.rsqrt(jnp.mean(oh * oh, axis=-1, keepdims=True) + EPS))
    g = g_ref[...]
    rec = jnp.concatenate(parts, axis=-1) * ng_ref[...] * (g * jax.nn.sigmoid(g))
    ch = conv_ref.shape[1]
    acc = jnp.dot(conv_ref[...], w_ref[0:ch, :], preferred_element_type=F32)
    acc = acc + jnp.dot(rec.astype(BF16), w_ref[ch:, :], preferred_element_type=F32)
    y_ref[...] = x_ref[...] + mod_ref[0, 2:3, :] * acc


def _out_proj(tok, mod, conv, o_f, o_b, px, norm_g, w, layout, skip_rows):
    rows, d = tok.shape
    ch = conv.shape[1]
    wr = o_f.shape[1]
    gcol = px.shape[1] // wr - 1
    tm = OUT_ROW_TILE
    off = skip_rows // tm
    assert conv.shape[0] == rows - skip_rows
    return pl.pallas_call(
        _out_kernel,
        out_shape=jax.ShapeDtypeStruct((rows - skip_rows, d), F32),
        grid=(rows // tm - off,),
        in_specs=[pl.BlockSpec((tm, d), lambda i: (i + off, 0)),
                  pl.BlockSpec((1, 6, d), lambda i: (_mod_row(i + off, tm, layout), 0, 0)),
                  pl.BlockSpec((tm, ch), lambda i: (i, 0)),
                  pl.BlockSpec((tm, wr), lambda i: (i + off, 0)),
                  pl.BlockSpec((tm, wr), lambda i: (i + off, 0)),
                  pl.BlockSpec((tm, wr), lambda i: (i + off, gcol)),
                  pl.BlockSpec((1, wr), lambda i: (0, 0)),
                  pl.BlockSpec((ch + wr, d), lambda i: (0, 0))],
        out_specs=pl.BlockSpec((tm, d), lambda i: (i, 0)),
        compiler_params=_cparams(("parallel",)),
        name="out_proj",
    )(tok, mod, conv, o_f, o_b, px, norm_g.reshape(1, wr), w)


def _ffn_kernel(x_ref, mod_ref, g_ref, w1_ref, w3_ref, w2_ref, y_ref, h_scr, acc_scr):
    f = pl.program_id(1)

    @pl.when(f == 0)
    def _():
        h = _norm_mod(x_ref[...], g_ref[...], mod_ref[0, 3:4, :], mod_ref[0, 4:5, :])
        h_scr[...] = h.astype(BF16)
        acc_scr[...] = jnp.zeros_like(acc_scr)

    h = h_scr[...]
    a = jnp.dot(h, w1_ref[...], preferred_element_type=F32)
    b = jnp.dot(h, w3_ref[...], preferred_element_type=F32)
    u = (a * jax.nn.sigmoid(a) * b).astype(BF16)
    acc_scr[...] += jnp.dot(u, w2_ref[...], preferred_element_type=F32)

    @pl.when(f == pl.num_programs(1) - 1)
    def _():
        y_ref[...] = x_ref[...] + mod_ref[0, 5:6, :] * acc_scr[...]


def _dense_ffn(tok, mod, g, w1, w3, w2, layout, tf=512):
    rows, d = tok.shape
    ff = w1.shape[1]
    tm = ROW_TILE
    return pl.pallas_call(
        _ffn_kernel,
        out_shape=jax.ShapeDtypeStruct((rows, d), F32),
        grid=(rows // tm, ff // tf),
        in_specs=[pl.BlockSpec((tm, d), lambda i, f: (i, 0)),
                  pl.BlockSpec((1, 6, d), lambda i, f: (_mod_row(i, tm, layout), 0, 0)),
                  pl.BlockSpec((1, d), lambda i, f: (0, 0)),
                  pl.BlockSpec((d, tf), lambda i, f: (0, f)),
                  pl.BlockSpec((d, tf), lambda i, f: (0, f)),
                  pl.BlockSpec((tf, d), lambda i, f: (f, 0))],
        out_specs=pl.BlockSpec((tm, d), lambda i, f: (i, 0)),
        scratch_shapes=[pltpu.VMEM((tm, d), BF16), pltpu.VMEM((tm, d), F32)],
        compiler_params=_cparams(("parallel", "arbitrary")),
        name="dense_ffn",
    )(tok, mod, g, w1, w3, w2)


def _router_kernel(x_ref, mod_ref, g_ref, wr_ref, h_ref, lg_ref):
    h = _norm_mod(x_ref[...], g_ref[...], mod_ref[0, 3:4, :], mod_ref[0, 4:5, :])
    h_ref[...] = h
    lg_ref[...] = jnp.dot(h, wr_ref[...], preferred_element_type=F32,
                          precision=lax.Precision.HIGHEST)


def _router(tok, mod, g, wr_pad, layout, skip_rows):
    rows, d = tok.shape
    tm = ROW_TILE
    off = skip_rows // tm
    nrow = rows - skip_rows
    ne = wr_pad.shape[1]
    return pl.pallas_call(
        _router_kernel,
        out_shape=(jax.ShapeDtypeStruct((nrow, d), F32), jax.ShapeDtypeStruct((nrow, ne), F32)),
        grid=(nrow // tm,),
        in_specs=[pl.BlockSpec((tm, d), lambda i: (i + off, 0)),
                  pl.BlockSpec((1, 6, d), lambda i: (_mod_row(i + off, tm, layout), 0, 0)),
                  pl.BlockSpec((1, d), lambda i: (0, 0)),
                  pl.BlockSpec((d, ne), lambda i: (0, 0))],
        out_specs=(pl.BlockSpec((tm, d), lambda i: (i, 0)),
                   pl.BlockSpec((tm, ne), lambda i: (i, 0))),
        compiler_params=_cparams(("parallel",)),
        name="moe_router",
    )(tok, mod, g, wr_pad)


def _row_copy(src_hbm, src_row, dst_vmem, dst_row, sem):
    return pltpu.make_async_copy(src_hbm.at[pl.ds(src_row, 1), :], dst_vmem.at[pl.ds(dst_row, 1), :], sem)


def _gather_start(idx_ref, idx_base, idx_stride, src_hbm, dst_vmem, nrows, sem, priorities=1):
    assert nrows % DMA_UNROLL == 0

    def issue(i, carry):
        for u in range(DMA_UNROLL):
            r = i * DMA_UNROLL + u
            _row_copy(src_hbm, idx_ref[idx_base + r * idx_stride], dst_vmem, r, sem).start(
                priority=u % priorities)
        return carry

    lax.fori_loop(0, nrows // DMA_UNROLL, issue, 0)


def _gather_wait(src_hbm, dst_vmem, nrows, sem):
    pltpu.make_async_copy(src_hbm.at[pl.ds(0, nrows), :], dst_vmem.at[pl.ds(0, nrows), :], sem).wait()


def _experts_kernel(blk_e_ref, nsub_ref, nb_ref, slot_ref, h_hbm, w1_ref, w3_ref, w2_ref, y_ref,
                    xf_scr, xb_scr, sem):
    g = pl.program_id(0)
    f = pl.program_id(1)
    used = g < nb_ref[0]

    @pl.when((f == 0) & jnp.logical_not(used))
    def _():
        y_ref[...] = jnp.zeros_like(y_ref)

    @pl.when(used)
    def _():
        for nsub in range(1, MOE_ROWS // MOE_SUB + 1):
            nrows = nsub * MOE_SUB

            @pl.when(nsub_ref[g] == nsub)
            def _():
                @pl.when(f == 0)
                def _():
                    _gather_start(slot_ref, g * MOE_ROWS, 1, h_hbm, xf_scr, nrows, sem)
                    _gather_wait(h_hbm, xf_scr, nrows, sem)
                    xb_scr[0:nrows, :] = xf_scr[0:nrows, :].astype(BF16)
                    y_ref[...] = jnp.zeros_like(y_ref)

                x = xb_scr[0:nrows, :]
                a = lax.dot_general(x, w1_ref[0], _NN, preferred_element_type=F32)
                b = lax.dot_general(x, w3_ref[0], _NN, preferred_element_type=F32)
                u = (a * jax.nn.sigmoid(a) * b).astype(BF16)
                y_ref[0:nrows, :] += lax.dot_general(u, w2_ref[0], _NN, preferred_element_type=F32)


def _experts(h, slot_tok, blk_e, nsub, nb_used, w1, w3, w2, tf=256):
    n, d = h.shape
    ff = w1.shape[2]
    nf = ff // tf
    n_blocks = slot_tok.shape[0] // MOE_ROWS

    def gidx(g, nb):
        return jnp.minimum(g, nb[0] - 1)

    def fidx(g, f, nb):
        return jnp.where(g < nb[0], f, nf - 1)

    grid_spec = pltpu.PrefetchScalarGridSpec(
        num_scalar_prefetch=4,
        grid=(n_blocks, nf),
        in_specs=[pl.BlockSpec(memory_space=pl.ANY),
                  pl.BlockSpec((1, d, tf), lambda g, f, be, ns, nb, st: (be[gidx(g, nb)], 0, fidx(g, f, nb))),
                  pl.BlockSpec((1, d, tf), lambda g, f, be, ns, nb, st: (be[gidx(g, nb)], 0, fidx(g, f, nb))),
                  pl.BlockSpec((1, tf, d), lambda g, f, be, ns, nb, st: (be[gidx(g, nb)], fidx(g, f, nb), 0))],
        out_specs=pl.BlockSpec((MOE_ROWS, d), lambda g, f, be, ns, nb, st: (g, 0)),
        scratch_shapes=[pltpu.VMEM((MOE_ROWS, d), F32),
                        pltpu.VMEM((MOE_ROWS, d), BF16),
                        pltpu.SemaphoreType.DMA],
    )
    return pl.pallas_call(
        _experts_kernel,
        out_shape=jax.ShapeDtypeStruct((n_blocks * MOE_ROWS, d), F32),
        grid_spec=grid_spec,
        compiler_params=_cparams(("arbitrary", "arbitrary")),
        name="moe_experts",
    )(blk_e, nsub, nb_used, slot_tok, h, w1, w3, w2)


def _combine_kernel(dest_ref, x_ref, mod_ref, w_ref, gf_ref, y_hbm, o_ref, y0_scr, y1_scr, sem):
    tm = x_ref.shape[0]
    base = pl.program_id(0) * tm * TOP_K
    _gather_start(dest_ref, base, TOP_K, y_hbm, y0_scr, tm, sem.at[0], priorities=2)
    _gather_start(dest_ref, base + 1, TOP_K, y_hbm, y1_scr, tm, sem.at[1], priorities=2)
    _gather_wait(y_hbm, y0_scr, tm, sem.at[0])
    _gather_wait(y_hbm, y1_scr, tm, sem.at[1])
    w = w_ref[...]
    y = w[:, 0:1] * y0_scr[...] + w[:, 1:2] * y1_scr[...]
    x = x_ref[...] + mod_ref[0, 5:6, :] * y
    o_ref[...] = x * lax.rsqrt(jnp.mean(x * x, axis=-1, keepdims=True) + EPS) * gf_ref[...]


def _combine_final(tok, mod, y, dest, top_w, g_final, layout, skip_rows):
    rows, d = tok.shape
    tm = ROW_TILE
    off = skip_rows // tm
    nrow = rows - skip_rows
    grid_spec = pltpu.PrefetchScalarGridSpec(
        num_scalar_prefetch=1,
        grid=(nrow // tm,),
        in_specs=[pl.BlockSpec((tm, d), lambda i, ds: (i + off, 0)),
                  pl.BlockSpec((1, 6, d), lambda i, ds: (_mod_row(i + off, tm, layout), 0, 0)),
                  pl.BlockSpec((tm, TOP_K), lambda i, ds: (i, 0)),
                  pl.BlockSpec((1, d), lambda i, ds: (0, 0)),
                  pl.BlockSpec(memory_space=pl.ANY)],
        out_specs=pl.BlockSpec((tm, d), lambda i, ds: (i, 0)),
        scratch_shapes=[pltpu.VMEM((tm, d), F32), pltpu.VMEM((tm, d), F32),
                        pltpu.SemaphoreType.DMA((TOP_K,))],
    )
    return pl.pallas_call(
        _combine_kernel,
        out_shape=jax.ShapeDtypeStruct((nrow, d), F32),
        grid_spec=grid_spec,
        compiler_params=_cparams(("arbitrary",)),
        name="moe_combine_final",
    )(dest, tok, mod, top_w, g_final.reshape(1, d), y)


def _route(logits):
    n = logits.shape[0]
    na = n * TOP_K
    top_logit, top_e = lax.top_k(logits, TOP_K)
    top_w = jax.nn.softmax(top_logit, axis=-1)
    a_e = top_e.reshape(-1).astype(I32)
    order = jnp.argsort(a_e, stable=True).astype(I32)
    inv = jnp.argsort(order).astype(I32)
    counts = jnp.sum(a_e[:, None] == jnp.arange(N_EXPERTS, dtype=I32)[None, :], axis=0).astype(I32)
    padded = (counts + MOE_ROWS - 1) // MOE_ROWS * MOE_ROWS
    pad_end = jnp.cumsum(padded)
    pad_start = pad_end - padded
    start = jnp.cumsum(counts) - counts
    dest = inv + (pad_start - start)[a_e]

    n_blocks = na // MOE_ROWS + N_EXPERTS
    slot = jnp.arange(n_blocks * MOE_ROWS, dtype=I32)
    slot_e = jnp.minimum(jnp.sum(slot[:, None] >= pad_end[None, :], axis=1), N_EXPERTS - 1).astype(I32)
    rank = slot - pad_start[slot_e]
    src = jnp.clip(start[slot_e] + rank, 0, na - 1)
    slot_tok = jnp.where(rank < counts[slot_e], order[src] // TOP_K, slot % n)

    blk0 = jnp.arange(n_blocks, dtype=I32) * MOE_ROWS
    blk_e = slot_e[::MOE_ROWS]
    blk_rows = jnp.clip(counts[blk_e] - (blk0 - pad_start[blk_e]), 0, MOE_ROWS)
    nb_used = (pad_end[-1:] // MOE_ROWS).astype(I32)
    nsub = jnp.where(blk0 < pad_end[-1], (blk_rows + MOE_SUB - 1) // MOE_SUB, 0).astype(I32)
    return top_w, slot_tok, dest, blk_e, nsub, nb_used


def kernel(x, c, ctx, c_ctx, w_ada, b_ada, g_mix, w_in, conv_w, conv_b, conv_ln_g, conv_ln_b, hgrn_lb_logits, hgrn_norm_g, w_out, g_ffn, ffn_w1, ffn_w3, ffn_w2, router_w, moe_w1, moe_w3, moe_w2, g_final):
    batch, seq, d = x.shape
    ctx_len = ctx.shape[1]
    depth = w_ada.shape[0]
    assert depth == 2 and batch == 2
    conv_ch = conv_w.shape[2]
    ctx_rows = batch * ctx_len
    layout = (ctx_rows, seq)

    cs = jnp.cumsum(jax.nn.softmax(hgrn_lb_logits.astype(F32), axis=1), axis=1)
    lb_all = cs - cs[:, :1]

    cc = jnp.zeros((MOD_ROWS, d), F32).at[0:batch].set(c).at[batch].set(c_ctx)
    mod_all = _adaln(cc, w_ada, b_ada).reshape(depth, MOD_ROWS, 6, d)

    tok = jnp.concatenate([ctx.reshape(ctx_rows, d), x.reshape(batch * seq, d)], axis=0)
    row = lambda a: a.reshape(1, -1)

    for l in range(depth):
        last = l == depth - 1
        mod = mod_all[l]
        px = _in_proj(tok, mod, row(g_mix[l]), w_in[l].astype(BF16), layout)
        o_f = _hgrn_scan(px, lb_all[0, l], False, batch, ctx_len, seq, 2 * conv_ch)
        o_b = _hgrn_scan(px, lb_all[1, l], True, batch, ctx_len, seq, 2 * conv_ch)
        conv = _conformer_conv(px, conv_w[l], conv_b[l], conv_ln_g[l], conv_ln_b[l], ctx_len,
                               with_ctx=not last, skip_blocks=ctx_rows // ctx_len)
        skip_rows = ctx_rows if last else 0
        tok = _out_proj(tok, mod, conv, o_f, o_b, px, hgrn_norm_g[l], w_out[l].astype(BF16), layout,
                        skip_rows)
        j = l // 2
        if l % 2 == 0:
            tok = _dense_ffn(tok, mod, row(g_ffn[l]), ffn_w1[j].astype(BF16), ffn_w3[j].astype(BF16),
                             ffn_w2[j].astype(BF16), layout)
        else:
            assert last
            wr_pad = jnp.zeros((d, 128), F32).at[:, :N_EXPERTS].set(router_w[j])
            lat_layout = (0, seq)
            h, logits = _router(tok, mod, row(g_ffn[l]), wr_pad, lat_layout, 0)
            top_w, slot_tok, dest, blk_e, nsub, nb_used = _route(logits[:, :N_EXPERTS])
            y = _experts(h, slot_tok, blk_e, nsub, nb_used, moe_w1[j], moe_w3[j], moe_w2[j])
            out = _combine_final(tok, mod, y, dest, top_w, g_final, lat_layout, 0)
    return out.reshape(batch, seq, d)
```

```python
import functools
import math

import numpy as np
import jax
import jax.numpy as jnp
from jax import lax
from jax.experimental import pallas as pl
from jax.experimental.pallas import tpu as pltpu

F32 = jnp.float32
BF16 = jnp.bfloat16
I32 = jnp.int32
EPS = 1e-6
LOG2E = math.log2(math.e)

SUBLANES = 8
GRID_W = 64
CONV_K = 31
CONV_PAD = 16
HG_HEADS = 8
HG_D = 128
HG_CHUNK = 128
N_EXPERTS = 8
TOP_K = 2
MOE_SUB = 256
MOE_ROWS = 4 * MOE_SUB
ROW_TILE = 512
OUT_ROW_TILE = 256
MOD_ROWS = 8
DMA_UNROLL = 8
V7X_VMEM_LIMIT = 56 * 1024 * 1024

_NN = (((1,), (0,)), ((), ()))
_NT = (((1,), (1,)), ((), ()))
_TN = (((0,), (0,)), ((), ()))


def _cparams(sem):
    return pltpu.CompilerParams(dimension_semantics=sem, vmem_limit_bytes=V7X_VMEM_LIMIT)


def _mod_row(i, tm, layout):
    ctx_rows, seq = layout
    assert ctx_rows % tm == 0 and seq % tm == 0
    return jnp.where(i < ctx_rows // tm, 2, (i - ctx_rows // tm) // (seq // tm))


def _norm_mod(x, g, shift, scale):
    y = x * lax.rsqrt(jnp.mean(x * x, axis=-1, keepdims=True) + EPS) * g
    return y * (1.0 + scale) + shift


def _ada_kernel(c_ref, w_ref, b_ref, o_ref):
    c = c_ref[...]
    a = c * jax.nn.sigmoid(c)
    o_ref[0] = jnp.dot(a, w_ref[0], preferred_element_type=F32) + b_ref[0]


def _adaln(cc, w_ada, b_ada, tn=1024):
    depth, d, n = w_ada.shape
    return pl.pallas_call(
        _ada_kernel,
        out_shape=jax.ShapeDtypeStruct((depth, MOD_ROWS, n), F32),
        grid=(depth, n // tn),
        in_specs=[pl.BlockSpec((MOD_ROWS, d), lambda l, j: (0, 0)),
                  pl.BlockSpec((1, d, tn), lambda l, j: (l, 0, j)),
                  pl.BlockSpec((1, 1, tn), lambda l, j: (l, 0, j))],
        out_specs=pl.BlockSpec((1, MOD_ROWS, tn), lambda l, j: (l, 0, j)),
        compiler_params=_cparams(("parallel", "parallel")),
        name="adaln",
    )(cc, w_ada, b_ada.reshape(depth, 1, n))


def _proj_kernel(x_ref, mod_ref, g_ref, w_ref, o_ref, h_scr):
    @pl.when(pl.program_id(1) == 0)
    def _():
        h = _norm_mod(x_ref[...], g_ref[...], mod_ref[0, 0:1, :], mod_ref[0, 1:2, :])
        h_scr[...] = h.astype(BF16)

    o_ref[...] = jnp.dot(h_scr[...], w_ref[0], preferred_element_type=F32)


def _in_proj(tok, mod, g, w, l, layout, tn=1792):
    rows, d = tok.shape
    n = w.shape[2]
    tm = ROW_TILE
    return pl.pallas_call(
        _proj_kernel,
        out_shape=jax.ShapeDtypeStruct((rows, n), F32),
        grid=(rows // tm, n // tn),
        in_specs=[pl.BlockSpec((tm, d), lambda i, j: (i, 0)),
                  pl.BlockSpec((1, 6, d), lambda i, j: (_mod_row(i, tm, layout), 0, 0)),
                  pl.BlockSpec((1, d), lambda i, j: (0, 0)),
                  pl.BlockSpec((1, d, tn), lambda i, j: (l, 0, j))],
        out_specs=pl.BlockSpec((tm, tn), lambda i, j: (i, j)),
        scratch_shapes=[pltpu.VMEM((tm, d), BF16)],
        compiler_params=_cparams(("parallel", "arbitrary")),
        name="in_proj",
    )(tok, mod, g, w)


def _conv_segment(u, w_ref, pad_scr, shift_scr, seg_len):
    ch = u.shape[1]
    zeros = jnp.zeros((CONV_PAD, ch), F32)
    pad_scr[0:CONV_PAD, :] = zeros
    pad_scr[CONV_PAD + seg_len:2 * CONV_PAD + seg_len, :] = zeros
    pad_scr[CONV_PAD:CONV_PAD + seg_len, :] = u
    base = CONV_PAD - CONV_K // 2
    span = seg_len + (CONV_K - 1) // SUBLANES * SUBLANES
    acc = jnp.zeros((seg_len, ch), F32)
    for r in range(SUBLANES):
        shift_scr[0:span, :] = pad_scr[base + r:base + r + span, :]
        for k in range(r, CONV_K, SUBLANES):
            acc = acc + shift_scr[k - r:k - r + seg_len, :] * w_ref[k:k + 1, :]
    return acc


def _conv_kernel(val_ref, gate_ref, w_ref, b_ref, lg_ref, lb_ref, o_ref, pad_scr, shift_scr, *,
                 ctx_blocks, ctx_len):
    rows = val_ref.shape[0]

    def run(seg_len):
        for s in range(rows // seg_len):
            sl = slice(s * seg_len, (s + 1) * seg_len)
            u = val_ref[sl, :] * jax.nn.sigmoid(gate_ref[sl, :])
            y = _conv_segment(u, w_ref, pad_scr, shift_scr, seg_len) + b_ref[...]
            mu = jnp.mean(y, axis=-1, keepdims=True)
            yc = y - mu
            var = jnp.mean(yc * yc, axis=-1, keepdims=True)
            z = yc * lax.rsqrt(var + EPS) * lg_ref[...] + lb_ref[...]
            o_ref[sl, :] = (z * jax.nn.sigmoid(z)).astype(o_ref.dtype)

    if ctx_blocks:
        is_ctx = pl.program_id(0) < ctx_blocks
        pl.when(is_ctx)(lambda: run(ctx_len))
        pl.when(jnp.logical_not(is_ctx))(lambda: run(GRID_W))
    else:
        run(GRID_W)


def _conformer_conv(px, w, b, lg, lb, ctx_len, with_ctx, skip_blocks):
    rows = px.shape[0]
    ch = w.shape[1]
    tm = ctx_len
    off = 0 if with_ctx else skip_blocks
    kern = functools.partial(_conv_kernel, ctx_blocks=skip_blocks if with_ctx else 0, ctx_len=ctx_len)
    vec = lambda a: a.reshape(1, ch)
    return pl.pallas_call(
        kern,
        out_shape=jax.ShapeDtypeStruct((rows - off * tm, ch), BF16),
        grid=(rows // tm - off,),
        in_specs=[pl.BlockSpec((tm, ch), lambda i: (i + off, 0)),
                  pl.BlockSpec((tm, ch), lambda i: (i + off, 1)),
                  pl.BlockSpec((CONV_K, ch), lambda i: (0, 0)),
                  pl.BlockSpec((1, ch), lambda i: (0, 0)),
                  pl.BlockSpec((1, ch), lambda i: (0, 0)),
                  pl.BlockSpec((1, ch), lambda i: (0, 0))],
        out_specs=pl.BlockSpec((tm, ch), lambda i: (i, 0)),
        scratch_shapes=[pltpu.VMEM((tm + 2 * CONV_PAD, ch), F32),
                        pltpu.VMEM((tm + 2 * CONV_PAD, ch), F32)],
        compiler_params=_cparams(("parallel",)),
        name="conformer_conv",
    )(px, px, w, vec(b), vec(lg), vec(lb))


def _hgrn_levels(c):
    return [c >> (i + 1) for i in range(c.bit_length() - 1)]


def _hgrn_constants(c, rev):
    p = np.arange(c)[::-1] if rev else np.arange(c)
    pt, ps = p[:, None], p[None, :]
    tri = (ps <= pt).astype(np.float32)
    masks = [((pt // (2 * m)) == (ps // (2 * m))) & ((pt // m) % 2 == 1) & ((ps // m) % 2 == 0)
             for m in _hgrn_levels(c)]
    return jnp.asarray(tri, dtype=BF16), jnp.asarray(np.stack(masks).astype(np.float32))


def _hgrn_neg_dist(bh, b_scr, sl, m, rev):
    c = bh.shape[0]
    off = m if rev else m - 1
    pieces = []
    if m >= SUBLANES:
        for j in range(c // (2 * m)):
            lo, mid, hi = j * 2 * m, j * 2 * m + m, (j + 1) * 2 * m
            ref = b_scr[lo + off:lo + off + 1, sl]
            first, second = (bh[lo:mid, :] - ref, ref - bh[mid:hi, :]) if rev else \
                            (ref - bh[lo:mid, :], bh[mid:hi, :] - ref)
            pieces += [first, second]
    elif 2 * m == SUBLANES:
        for g in range(c // SUBLANES):
            r = g * SUBLANES + off
            pieces.append(-jnp.abs(bh[g * SUBLANES:(g + 1) * SUBLANES, :] - b_scr[r:r + 1, sl]))
    else:
        assert 4 * m == SUBLANES
        upper = lax.broadcasted_iota(I32, (SUBLANES, bh.shape[1]), 0) < 2 * m
        for g in range(c // SUBLANES):
            r = g * SUBLANES + off
            ref = jnp.where(upper, b_scr[r:r + 1, sl], b_scr[r + 2 * m:r + 2 * m + 1, sl])
            pieces.append(-jnp.abs(bh[g * SUBLANES:(g + 1) * SUBLANES, :] - ref))
    return jnp.concatenate(pieces, axis=0)


def _hgrn_chunk(q_ref, z_ref, v_ref, lb_ref, tri_ref, mask_ref, o_ref, st_scr, f_scr, b_scr, rev):
    c = q_ref.shape[0]
    lb = lb_ref[...]
    f = lb + (1.0 - lb) * jax.nn.sigmoid(z_ref[...])
    f_scr[...] = f
    lf = jnp.log(f) * LOG2E
    lf_hi = lf.astype(BF16)
    lf_lo = (lf - lf_hi.astype(F32)).astype(BF16)
    tri = tri_ref[...]
    b_scr[...] = (jnp.dot(tri, lf_hi, preferred_element_type=F32)
                  + jnp.dot(tri, lf_lo, preferred_element_type=F32))
    last = 0 if rev else c - 1
    levels = _hgrn_levels(c)

    for h in range(q_ref.shape[1] // HG_D):
        sl = slice(h * HG_D, (h + 1) * HG_D)
        qh = q_ref[:, sl]
        vh = v_ref[:, sl]
        fh = f_scr[:, sl]
        kh = 1.0 - fh
        bh = b_scr[:, sl]
        b_last = b_scr[last:last + 1, sl]
        st = st_scr[h]
        vb = vh.astype(BF16)

        o = lax.dot_general((qh * jnp.exp2(bh)).astype(BF16), st.astype(BF16), _NT,
                            preferred_element_type=F32)
        scores = jnp.zeros((c, c), F32)
        for li, m in enumerate(levels):
            if m == 1:
                qm, km = qh * fh, kh
            else:
                e = jnp.exp2(_hgrn_neg_dist(bh, b_scr, sl, m, rev))
                qm, km = qh * e, kh * e
            sc = lax.dot_general(qm.astype(BF16), km.astype(BF16), _NT, preferred_element_type=F32)
            scores = scores + sc * mask_ref[li]
        o = o + jnp.dot(scores.astype(BF16), vb, preferred_element_type=F32)
        o = o + jnp.sum(qh * kh, axis=-1, keepdims=True) * vh
        o_ref[:, sl] = o

        ke = (kh * jnp.exp2(b_last - bh)).astype(BF16)
        st_scr[h] = st * jnp.exp2(b_last) + lax.dot_general(vb, ke, _TN, preferred_element_type=F32)


def _hgrn_kernel(*refs):
    ins, outs, (st_scr, f_scr, b_scr) = refs[:12], refs[12:14], refs[14:]

    @pl.when(pl.program_id(1) == 0)
    def _():
        st_scr[...] = jnp.zeros_like(st_scr)

    for dr in range(2):
        _hgrn_chunk(*ins[6 * dr:6 * dr + 6], outs[dr], st_scr.at[dr], f_scr.at[dr], b_scr.at[dr],
                    rev=dr == 1)


def _hgrn_scan(px, lb, batch, ctx_len, seq_len, col0):
    rows = px.shape[0]
    c = HG_CHUNK
    w = HG_HEADS * HG_D
    cb = col0 // w
    ncx, nl = ctx_len // c, seq_len // c
    lat0 = batch * ncx

    def row_block(rev, b, n):
        nc = ncx - 1 - n if rev else n
        nx = nl - 1 - (n - ncx) if rev else n - ncx
        return jnp.where(n < ncx, b * ncx + nc, lat0 + b * nl + nx)

    operands, in_specs, out_specs = [], [], []
    for rev in (False, True):
        rb = functools.partial(row_block, rev)
        tri, masks = _hgrn_constants(c, rev)
        zcol = cb + (2 if rev else 1)
        dr = int(rev)
        operands += [px, px, px, lb.reshape(2, 1, w), tri, masks]
        in_specs += [pl.BlockSpec((c, w), lambda b, n, rb=rb: (rb(b, n), cb)),
                     pl.BlockSpec((c, w), lambda b, n, rb=rb, zcol=zcol: (rb(b, n), zcol)),
                     pl.BlockSpec((c, w), lambda b, n, rb=rb: (rb(b, n), cb + 3)),
                     pl.BlockSpec((None, 1, w), lambda b, n, dr=dr: (dr, 0, 0)),
                     pl.BlockSpec((c, c), lambda b, n: (0, 0)),
                     pl.BlockSpec(masks.shape, lambda b, n: (0, 0, 0))]
        out_specs.append(pl.BlockSpec((c, w), lambda b, n, rb=rb: (rb(b, n), 0)))
    return pl.pallas_call(
        _hgrn_kernel,
        out_shape=(jax.ShapeDtypeStruct((rows, w), F32),) * 2,
        grid=(batch, ncx + nl),
        in_specs=in_specs,
        out_specs=tuple(out_specs),
        scratch_shapes=[pltpu.VMEM((2, HG_HEADS, HG_D, HG_D), F32),
                        pltpu.VMEM((2, c, w), F32),
                        pltpu.VMEM((2, c, w), F32)],
        compiler_params=_cparams(("parallel", "arbitrary")),
        name="hgrn_scan",
    )(*operands)


def _out_kernel(x_ref, mod_ref, conv_ref, of_ref, ob_ref, g_ref, ng_ref, w_ref, y_ref):
    o = of_ref[...] + ob_ref[...]
    parts = []
    for h in range(HG_HEADS):
        oh = o[:, h * HG_D:(h + 1) * HG_D]
        parts.append(oh * lax.rsqrt(jnp.mean(oh * oh, axis=-1, keepdims=True) + EPS))
    g = g_ref[...]
    rec = jnp.concatenate(parts, axis=-1) * ng_ref[...] * (g * jax.nn.sigmoid(g))
    ch = conv_ref.shape[1]
    acc = jnp.dot(conv_ref[...], w_ref[0, 0:ch, :], preferred_element_type=F32)
    acc = acc + jnp.dot(rec.astype(BF16), w_ref[0, ch:, :], preferred_element_type=F32)
    y_ref[...] = x_ref[...] + mod_ref[0, 2:3, :] * acc


def _out_proj(tok, mod, conv, o_f, o_b, px, norm_g, w, l, layout, skip_rows):
    rows, d = tok.shape
    ch = conv.shape[1]
    wr = o_f.shape[1]
    gcol = px.shape[1] // wr - 1
    tm = OUT_ROW_TILE
    off = skip_rows // tm
    assert conv.shape[0] == rows - skip_rows
    return pl.pallas_call(
        _out_kernel,
        out_shape=jax.ShapeDtypeStruct((rows - skip_rows, d), F32),
        grid=(rows // tm - off,),
        in_specs=[pl.BlockSpec((tm, d), lambda i: (i + off, 0)),
                  pl.BlockSpec((1, 6, d), lambda i: (_mod_row(i + off, tm, layout), 0, 0)),
                  pl.BlockSpec((tm, ch), lambda i: (i, 0)),
                  pl.BlockSpec((tm, wr), lambda i: (i + off, 0)),
                  pl.BlockSpec((tm, wr), lambda i: (i + off, 0)),
                  pl.BlockSpec((tm, wr), lambda i: (i + off, gcol)),
                  pl.BlockSpec((1, wr), lambda i: (0, 0)),
                  pl.BlockSpec((1, ch + wr, d), lambda i: (l, 0, 0))],
        out_specs=pl.BlockSpec((tm, d), lambda i: (i, 0)),
        compiler_params=_cparams(("parallel",)),
        name="out_proj",
    )(tok, mod, conv, o_f, o_b, px, norm_g.reshape(1, wr), w)


def _ffn_kernel(x_ref, mod_ref, g_ref, w1_ref, w3_ref, w2_ref, y_ref, h_scr, acc_scr):
    f = pl.program_id(1)

    @pl.when(f == 0)
    def _():
        h = _norm_mod(x_ref[...], g_ref[...], mod_ref[0, 3:4, :], mod_ref[0, 4:5, :])
        h_scr[...] = h.astype(BF16)
        acc_scr[...] = jnp.zeros_like(acc_scr)

    h = h_scr[...]
    a = jnp.dot(h, w1_ref[...], preferred_element_type=F32)
    b = jnp.dot(h, w3_ref[...], preferred_element_type=F32)
    u = (a * jax.nn.sigmoid(a) * b).astype(BF16)
    acc_scr[...] += jnp.dot(u, w2_ref[...], preferred_element_type=F32)

    @pl.when(f == pl.num_programs(1) - 1)
    def _():
        y_ref[...] = x_ref[...] + mod_ref[0, 5:6, :] * acc_scr[...]


def _dense_ffn(tok, mod, g, w1, w3, w2, layout, tf=512):
    rows, d = tok.shape
    ff = w1.shape[1]
    tm = ROW_TILE
    return pl.pallas_call(
        _ffn_kernel,
        out_shape=jax.ShapeDtypeStruct((rows, d), F32),
        grid=(rows // tm, ff // tf),
        in_specs=[pl.BlockSpec((tm, d), lambda i, f: (i, 0)),
                  pl.BlockSpec((1, 6, d), lambda i, f: (_mod_row(i, tm, layout), 0, 0)),
                  pl.BlockSpec((1, d), lambda i, f: (0, 0)),
                  pl.BlockSpec((d, tf), lambda i, f: (0, f)),
                  pl.BlockSpec((d, tf), lambda i, f: (0, f)),
                  pl.BlockSpec((tf, d), lambda i, f: (f, 0))],
        out_specs=pl.BlockSpec((tm, d), lambda i, f: (i, 0)),
        scratch_shapes=[pltpu.VMEM((tm, d), BF16), pltpu.VMEM((tm, d), F32)],
        compiler_params=_cparams(("parallel", "arbitrary")),
        name="dense_ffn",
    )(tok, mod, g, w1, w3, w2)


def _router_kernel(x_ref, mod_ref, g_ref, wr_ref, h_ref, lg_ref):
    h = _norm_mod(x_ref[...], g_ref[...], mod_ref[0, 3:4, :], mod_ref[0, 4:5, :])
    h_ref[...] = h
    lg_ref[...] = jnp.dot(h, wr_ref[...], preferred_element_type=F32,
                          precision=lax.Precision.HIGHEST)


def _router(tok, mod, g, wr_pad, layout, skip_rows):
    rows, d = tok.shape
    tm = ROW_TILE
    off = skip_rows // tm
    nrow = rows - skip_rows
    ne = wr_pad.shape[1]
    return pl.pallas_call(
        _router_kernel,
        out_shape=(jax.ShapeDtypeStruct((nrow, d), F32), jax.ShapeDtypeStruct((nrow, ne), F32)),
        grid=(nrow // tm,),
        in_specs=[pl.BlockSpec((tm, d), lambda i: (i + off, 0)),
                  pl.BlockSpec((1, 6, d), lambda i: (_mod_row(i + off, tm, layout), 0, 0)),
                  pl.BlockSpec((1, d), lambda i: (0, 0)),
                  pl.BlockSpec((d, ne), lambda i: (0, 0))],
        out_specs=(pl.BlockSpec((tm, d), lambda i: (i, 0)),
                   pl.BlockSpec((tm, ne), lambda i: (i, 0))),
        compiler_params=_cparams(("parallel",)),
        name="moe_router",
    )(tok, mod, g, wr_pad)


def _row_copy(src_hbm, src_row, dst_vmem, dst_row, sem):
    return pltpu.make_async_copy(src_hbm.at[pl.ds(src_row, 1), :], dst_vmem.at[pl.ds(dst_row, 1), :], sem)


def _gather_start(idx_ref, idx_base, idx_stride, src_hbm, dst_vmem, nrows, sem, priorities=1):
    assert nrows % DMA_UNROLL == 0

    def issue(i, carry):
        for u in range(DMA_UNROLL):
            r = i * DMA_UNROLL + u
            _row_copy(src_hbm, idx_ref[idx_base + r * idx_stride], dst_vmem, r, sem).start(
                priority=u % priorities)
        return carry

    lax.fori_loop(0, nrows // DMA_UNROLL, issue, 0)


def _gather_wait(src_hbm, dst_vmem, nrows, sem):
    pltpu.make_async_copy(src_hbm.at[pl.ds(0, nrows), :], dst_vmem.at[pl.ds(0, nrows), :], sem).wait()


def _experts_kernel(blk_e_ref, nsub_ref, nb_ref, slot_ref, h_hbm, w1_ref, w3_ref, w2_ref, y_ref,
                    xb_scr, sem):
    g = pl.program_id(0)
    f = pl.program_id(1)
    used = g < nb_ref[0]

    @pl.when((f == 0) & jnp.logical_not(used))
    def _():
        y_ref[...] = jnp.zeros_like(y_ref)

    @pl.when(used)
    def _():
        for nsub in range(1, MOE_ROWS // MOE_SUB + 1):
            nrows = nsub * MOE_SUB

            @pl.when(nsub_ref[g] == nsub)
            def _():
                @pl.when(f == 0)
                def _():
                    _gather_start(slot_ref, g * MOE_ROWS, 1, h_hbm, y_ref, nrows, sem)
                    _gather_wait(h_hbm, y_ref, nrows, sem)
                    xb_scr[0:nrows, :] = y_ref[0:nrows, :].astype(BF16)
                    y_ref[...] = jnp.zeros_like(y_ref)

                x = xb_scr[0:nrows, :]
                a = lax.dot_general(x, w1_ref[0], _NN, preferred_element_type=F32)
                b = lax.dot_general(x, w3_ref[0], _NN, preferred_element_type=F32)
                u = (a * jax.nn.sigmoid(a) * b).astype(BF16)
                y_ref[0:nrows, :] += lax.dot_general(u, w2_ref[0], _NN, preferred_element_type=F32)


def _experts(h, slot_tok, blk_e, nsub, nb_used, w1, w3, w2, tf=512):
    n, d = h.shape
    ff = w1.shape[2]
    nf = ff // tf
    n_blocks = slot_tok.shape[0] // MOE_ROWS

    def gidx(g, nb):
        return jnp.minimum(g, nb[0] - 1)

    def fidx(g, f, nb):
        return jnp.where(g < nb[0], f, nf - 1)

    grid_spec = pltpu.PrefetchScalarGridSpec(
        num_scalar_prefetch=4,
        grid=(n_blocks, nf),
        in_specs=[pl.BlockSpec(memory_space=pl.ANY),
                  pl.BlockSpec((1, d, tf), lambda g, f, be, ns, nb, st: (be[gidx(g, nb)], 0, fidx(g, f, nb))),
                  pl.BlockSpec((1, d, tf), lambda g, f, be, ns, nb, st: (be[gidx(g, nb)], 0, fidx(g, f, nb))),
                  pl.BlockSpec((1, tf, d), lambda g, f, be, ns, nb, st: (be[gidx(g, nb)], fidx(g, f, nb), 0))],
        out_specs=pl.BlockSpec((MOE_ROWS, d), lambda g, f, be, ns, nb, st: (g, 0)),
        scratch_shapes=[pltpu.VMEM((MOE_ROWS, d), BF16),
                        pltpu.SemaphoreType.DMA],
    )
    return pl.pallas_call(
        _experts_kernel,
        out_shape=jax.ShapeDtypeStruct((n_blocks * MOE_ROWS, d), F32),
        grid_spec=grid_spec,
        compiler_params=_cparams(("arbitrary", "arbitrary")),
        name="moe_experts",
    )(blk_e, nsub, nb_used, slot_tok, h, w1, w3, w2)


def _combine_kernel(dest_ref, x_ref, mod_ref, w_ref, gf_ref, y_hbm, o_ref, y0_scr, y1_scr, sem):
    tm = x_ref.shape[0]
    base = pl.program_id(0) * tm * TOP_K
    _gather_start(dest_ref, base, TOP_K, y_hbm, y0_scr, tm, sem.at[0], priorities=2)
    _gather_start(dest_ref, base + 1, TOP_K, y_hbm, y1_scr, tm, sem.at[1], priorities=2)
    _gather_wait(y_hbm, y0_scr, tm, sem.at[0])
    _gather_wait(y_hbm, y1_scr, tm, sem.at[1])
    w = w_ref[...]
    y = w[:, 0:1] * y0_scr[...] + w[:, 1:2] * y1_scr[...]
    x = x_ref[...] + mod_ref[0, 5:6, :] * y
    o_ref[...] = x * lax.rsqrt(jnp.mean(x * x, axis=-1, keepdims=True) + EPS) * gf_ref[...]


def _combine_final(tok, mod, y, dest, top_w, g_final, layout, skip_rows):
    rows, d = tok.shape
    tm = ROW_TILE
    off = skip_rows // tm
    nrow = rows - skip_rows
    grid_spec = pltpu.PrefetchScalarGridSpec(
        num_scalar_prefetch=1,
        grid=(nrow // tm,),
        in_specs=[pl.BlockSpec((tm, d), lambda i, ds: (i + off, 0)),
                  pl.BlockSpec((1, 6, d), lambda i, ds: (_mod_row(i + off, tm, layout), 0, 0)),
                  pl.BlockSpec((tm, TOP_K), lambda i, ds: (i, 0)),
                  pl.BlockSpec((1, d), lambda i, ds: (0, 0)),
                  pl.BlockSpec(memory_space=pl.ANY)],
        out_specs=pl.BlockSpec((tm, d), lambda i, ds: (i, 0)),
        scratch_shapes=[pltpu.VMEM((tm, d), F32), pltpu.VMEM((tm, d), F32),
                        pltpu.SemaphoreType.DMA((TOP_K,))],
    )
    return pl.pallas_call(
        _combine_kernel,
        out_shape=jax.ShapeDtypeStruct((nrow, d), F32),
        grid_spec=grid_spec,
        compiler_params=_cparams(("arbitrary",)),
        name="moe_combine_final",
    )(dest, tok, mod, top_w, g_final.reshape(1, d), y)


def _route(logits):
    n = logits.shape[0]
    na = n * TOP_K
    top_logit, top_e = lax.top_k(logits, TOP_K)
    top_w = jax.nn.softmax(top_logit, axis=-1)
    a_e = top_e.reshape(-1).astype(I32)
    order = jnp.argsort(a_e, stable=True).astype(I32)
    inv = jnp.argsort(order).astype(I32)
    counts = jnp.sum(a_e[:, None] == jnp.arange(N_EXPERTS, dtype=I32)[None, :], axis=0).astype(I32)
    padded = (counts + MOE_ROWS - 1) // MOE_ROWS * MOE_ROWS
    pad_end = jnp.cumsum(padded)
    pad_start = pad_end - padded
    start = jnp.cumsum(counts) - counts
    dest = inv + (pad_start - start)[a_e]

    n_blocks = na // MOE_ROWS + N_EXPERTS
    slot = jnp.arange(n_blocks * MOE_ROWS, dtype=I32)
    slot_e = jnp.minimum(jnp.sum(slot[:, None] >= pad_end[None, :], axis=1), N_EXPERTS - 1).astype(I32)
    rank = slot - pad_start[slot_e]
    src = jnp.clip(start[slot_e] + rank, 0, na - 1)
    slot_tok = jnp.where(rank < counts[slot_e], order[src] // TOP_K, slot % n)

    blk0 = jnp.arange(n_blocks, dtype=I32) * MOE_ROWS
    blk_e = slot_e[::MOE_ROWS]
    blk_rows = jnp.clip(counts[blk_e] - (blk0 - pad_start[blk_e]), 0, MOE_ROWS)
    nb_used = (pad_end[-1:] // MOE_ROWS).astype(I32)
    nsub = jnp.where(blk0 < pad_end[-1], (blk_rows + MOE_SUB - 1) // MOE_SUB, 0).astype(I32)
    return top_w, slot_tok, dest, blk_e, nsub, nb_used


def kernel(x, c, ctx, c_ctx, w_ada, b_ada, g_mix, w_in, conv_w, conv_b, conv_ln_g, conv_ln_b, hgrn_lb_logits, hgrn_norm_g, w_out, g_ffn, ffn_w1, ffn_w3, ffn_w2, router_w, moe_w1, moe_w3, moe_w2, g_final):
    batch, seq, d = x.shape
    ctx_len = ctx.shape[1]
    depth = w_ada.shape[0]
    assert depth == 2 and batch == 2
    conv_ch = conv_w.shape[2]
    ctx_rows = batch * ctx_len
    layout = (ctx_rows, seq)

    cs = jnp.cumsum(jax.nn.softmax(hgrn_lb_logits.astype(F32), axis=1), axis=1)
    lb_all = cs - cs[:, :1]

    cc = jnp.zeros((MOD_ROWS, d), F32).at[0:batch].set(c).at[batch].set(c_ctx)
    mod_all = _adaln(cc, w_ada, b_ada).reshape(depth, MOD_ROWS, 6, d)

    tok = jnp.concatenate([ctx.reshape(ctx_rows, d), x.reshape(batch * seq, d)], axis=0)
    row = lambda a: a.reshape(1, -1)
    w_in_b = w_in.astype(BF16)
    w_out_b = w_out.astype(BF16)

    for l in range(depth):
        last = l == depth - 1
        mod = mod_all[l]
        px = _in_proj(tok, mod, row(g_mix[l]), w_in_b, l, layout)
        o_f, o_b = _hgrn_scan(px, lb_all[:, l], batch, ctx_len, seq, 2 * conv_ch)
        conv = _conformer_conv(px, conv_w[l], conv_b[l], conv_ln_g[l], conv_ln_b[l], ctx_len,
                               with_ctx=not last, skip_blocks=ctx_rows // ctx_len)
        skip_rows = ctx_rows if last else 0
        tok = _out_proj(tok, mod, conv, o_f, o_b, px, hgrn_norm_g[l], w_out_b, l, layout, skip_rows)
        j = l // 2
        if l % 2 == 0:
            tok = _dense_ffn(tok, mod, row(g_ffn[l]), ffn_w1[j].astype(BF16), ffn_w3[j].astype(BF16),
                             ffn_w2[j].astype(BF16), layout)
        else:
            assert last
            wr_pad = jnp.zeros((d, 128), F32).at[:, :N_EXPERTS].set(router_w[j])
            lat_layout = (0, seq)
            h, logits = _router(tok, mod, row(g_ffn[l]), wr_pad, lat_layout, 0)
            top_w, slot_tok, dest, blk_e, nsub, nb_used = _route(logits[:, :N_EXPERTS])
            y = _experts(h, slot_tok, blk_e, nsub, nb_used, moe_w1[j], moe_w3[j], moe_w2[j])
            out = _combine_final(tok, mod, y, dest, top_w, g_final, lat_layout, 0)
    return out.reshape(batch, seq, d)
```

```python
import functools
import math

import numpy as np
import jax
import jax.numpy as jnp
from jax import lax
from jax.experimental import pallas as pl
from jax.experimental.pallas import tpu as pltpu

F32 = jnp.float32
BF16 = jnp.bfloat16
I32 = jnp.int32
EPS = 1e-6
LOG2E = math.log2(math.e)

SUBLANES = 8
GRID_W = 64
CONV_K = 31
CONV_PAD = 16
HG_HEADS = 8
HG_D = 128
HG_CHUNK = 128
N_EXPERTS = 8
TOP_K = 2
MOE_SUB = 256
MOE_ROWS = 4 * MOE_SUB
ROW_TILE = 512
OUT_ROW_TILE = 512
MOD_ROWS = 8
DMA_UNROLL = 8
V7X_VMEM_LIMIT = 56 * 1024 * 1024

_NN = (((1,), (0,)), ((), ()))
_NT = (((1,), (1,)), ((), ()))
_TN = (((0,), (0,)), ((), ()))


def _cparams(sem):
    return pltpu.CompilerParams(dimension_semantics=sem, vmem_limit_bytes=V7X_VMEM_LIMIT)


def _mod_row(i, tm, layout):
    ctx_rows, seq = layout
    assert ctx_rows % tm == 0 and seq % tm == 0
    return jnp.where(i < ctx_rows // tm, 2, (i - ctx_rows // tm) // (seq // tm))


def _norm_mod(x, g, shift, scale):
    y = x * lax.rsqrt(jnp.mean(x * x, axis=-1, keepdims=True) + EPS) * g
    return y * (1.0 + scale) + shift


def _ada_kernel(c_ref, w_ref, b_ref, o_ref):
    c = c_ref[...]
    a = c * jax.nn.sigmoid(c)
    o_ref[0] = jnp.dot(a, w_ref[0], preferred_element_type=F32) + b_ref[0]


def _adaln(cc, w_ada, b_ada, tn=1024):
    depth, d, n = w_ada.shape
    return pl.pallas_call(
        _ada_kernel,
        out_shape=jax.ShapeDtypeStruct((depth, MOD_ROWS, n), F32),
        grid=(depth, n // tn),
        in_specs=[pl.BlockSpec((MOD_ROWS, d), lambda l, j: (0, 0)),
                  pl.BlockSpec((1, d, tn), lambda l, j: (l, 0, j)),
                  pl.BlockSpec((1, 1, tn), lambda l, j: (l, 0, j))],
        out_specs=pl.BlockSpec((1, MOD_ROWS, tn), lambda l, j: (l, 0, j)),
        compiler_params=_cparams(("parallel", "parallel")),
        name="adaln",
    )(cc, w_ada, b_ada.reshape(depth, 1, n))


def _pair_dispatch(n_half, run):
    i = pl.program_id(0)
    if n_half % 2:
        pl.when(i < n_half // 2)(lambda: run(2))
        pl.when(i == n_half // 2)(lambda: run(1))
    else:
        run(2)


def _pair_specs(d, layout, n_half, index):
    halves = [lambda i: 2 * i, lambda i: jnp.minimum(2 * i + 1, n_half - 1)]
    rows = [pl.BlockSpec((ROW_TILE, d), index(lambda i, h=h: (h(i), 0))) for h in halves]
    mods = [pl.BlockSpec((1, 6, d), index(lambda i, h=h: (_mod_row(h(i), ROW_TILE, layout), 0, 0)))
            for h in halves]
    return rows + mods


def _proj_kernel(xa_ref, xb_ref, ma_ref, mb_ref, g_ref, w_ref, o_ref, h_scr, *, n_half):
    tm = xa_ref.shape[0]
    first_col = pl.program_id(1) == 0

    def run(halves):
        rows = halves * tm

        @pl.when(first_col)
        def _():
            for k, (x_ref, m_ref) in enumerate(((xa_ref, ma_ref), (xb_ref, mb_ref))[:halves]):
                h = _norm_mod(x_ref[...], g_ref[...], m_ref[0, 0:1, :], m_ref[0, 1:2, :])
                h_scr[k * tm:(k + 1) * tm, :] = h.astype(BF16)

        o_ref[0:rows, :] = jnp.dot(h_scr[0:rows, :], w_ref[0], preferred_element_type=F32)

    _pair_dispatch(n_half, run)


def _in_proj(tok, mod, g, w, l, layout, tn=1024):
    rows, d = tok.shape
    n = w.shape[2]
    n_half = rows // ROW_TILE
    index = lambda fn: (lambda i, j: fn(i))
    return pl.pallas_call(
        functools.partial(_proj_kernel, n_half=n_half),
        out_shape=jax.ShapeDtypeStruct((rows, n), F32),
        grid=(pl.cdiv(n_half, 2), n // tn),
        in_specs=_pair_specs(d, layout, n_half, index) + [
            pl.BlockSpec((1, d), lambda i, j: (0, 0)),
            pl.BlockSpec((1, d, tn), lambda i, j: (l, 0, j))],
        out_specs=pl.BlockSpec((2 * ROW_TILE, tn), lambda i, j: (i, j)),
        scratch_shapes=[pltpu.VMEM((2 * ROW_TILE, d), BF16)],
        compiler_params=_cparams(("parallel", "arbitrary")),
        name="in_proj",
    )(tok, tok, mod, mod, g, w)


def _conv_segment(u, w_ref, pad_scr, shift_scr, seg_len):
    ch = u.shape[1]
    zeros = jnp.zeros((CONV_PAD, ch), F32)
    pad_scr[0:CONV_PAD, :] = zeros
    pad_scr[CONV_PAD + seg_len:2 * CONV_PAD + seg_len, :] = zeros
    pad_scr[CONV_PAD:CONV_PAD + seg_len, :] = u
    base = CONV_PAD - CONV_K // 2
    span = seg_len + (CONV_K - 1) // SUBLANES * SUBLANES
    acc = jnp.zeros((seg_len, ch), F32)
    for r in range(SUBLANES):
        shift_scr[0:span, :] = pad_scr[base + r:base + r + span, :]
        for k in range(r, CONV_K, SUBLANES):
            acc = acc + shift_scr[k - r:k - r + seg_len, :] * w_ref[k:k + 1, :]
    return acc


def _conv_kernel(val_ref, gate_ref, w_ref, b_ref, lg_ref, lb_ref, o_ref, pad_scr, shift_scr, *,
                 ctx_blocks, ctx_len):
    rows = val_ref.shape[0]

    def run(seg_len):
        for s in range(rows // seg_len):
            sl = slice(s * seg_len, (s + 1) * seg_len)
            u = val_ref[sl, :] * jax.nn.sigmoid(gate_ref[sl, :])
            y = _conv_segment(u, w_ref, pad_scr, shift_scr, seg_len) + b_ref[...]
            mu = jnp.mean(y, axis=-1, keepdims=True)
            yc = y - mu
            var = jnp.mean(yc * yc, axis=-1, keepdims=True)
            z = yc * lax.rsqrt(var + EPS) * lg_ref[...] + lb_ref[...]
            o_ref[sl, :] = (z * jax.nn.sigmoid(z)).astype(o_ref.dtype)

    if ctx_blocks:
        is_ctx = pl.program_id(0) < ctx_blocks
        pl.when(is_ctx)(lambda: run(ctx_len))
        pl.when(jnp.logical_not(is_ctx))(lambda: run(GRID_W))
    else:
        run(GRID_W)


def _conformer_conv(px, w, b, lg, lb, ctx_len, with_ctx, skip_blocks):
    rows = px.shape[0]
    ch = w.shape[1]
    tm = ctx_len
    off = 0 if with_ctx else skip_blocks
    kern = functools.partial(_conv_kernel, ctx_blocks=skip_blocks if with_ctx else 0, ctx_len=ctx_len)
    vec = lambda a: a.reshape(1, ch)
    return pl.pallas_call(
        kern,
        out_shape=jax.ShapeDtypeStruct((rows - off * tm, ch), BF16),
        grid=(rows // tm - off,),
        in_specs=[pl.BlockSpec((tm, ch), lambda i: (i + off, 0)),
                  pl.BlockSpec((tm, ch), lambda i: (i + off, 1)),
                  pl.BlockSpec((CONV_K, ch), lambda i: (0, 0)),
                  pl.BlockSpec((1, ch), lambda i: (0, 0)),
                  pl.BlockSpec((1, ch), lambda i: (0, 0)),
                  pl.BlockSpec((1, ch), lambda i: (0, 0))],
        out_specs=pl.BlockSpec((tm, ch), lambda i: (i, 0)),
        scratch_shapes=[pltpu.VMEM((tm + 2 * CONV_PAD, ch), F32),
                        pltpu.VMEM((tm + 2 * CONV_PAD, ch), F32)],
        compiler_params=_cparams(("parallel",)),
        name="conformer_conv",
    )(px, px, w, vec(b), vec(lg), vec(lb))


def _hgrn_levels(c):
    return [c >> (i + 1) for i in range(c.bit_length() - 1)]


def _hgrn_constants(c, rev):
    p = np.arange(c)[::-1] if rev else np.arange(c)
    pt, ps = p[:, None], p[None, :]
    tri = (ps <= pt).astype(np.float32)
    masks = [((pt // (2 * m)) == (ps // (2 * m))) & ((pt // m) % 2 == 1) & ((ps // m) % 2 == 0)
             for m in _hgrn_levels(c)]
    return jnp.asarray(tri, dtype=BF16), jnp.asarray(np.stack(masks).astype(np.float32))


def _hgrn_neg_dist(bh, b_scr, sl, m, rev):
    c = bh.shape[0]
    off = m if rev else m - 1
    pieces = []
    if m >= SUBLANES:
        for j in range(c // (2 * m)):
            lo, mid, hi = j * 2 * m, j * 2 * m + m, (j + 1) * 2 * m
            ref = b_scr[lo + off:lo + off + 1, sl]
            first, second = (bh[lo:mid, :] - ref, ref - bh[mid:hi, :]) if rev else \
                            (ref - bh[lo:mid, :], bh[mid:hi, :] - ref)
            pieces += [first, second]
    elif 2 * m == SUBLANES:
        for g in range(c // SUBLANES):
            r = g * SUBLANES + off
            pieces.append(-jnp.abs(bh[g * SUBLANES:(g + 1) * SUBLANES, :] - b_scr[r:r + 1, sl]))
    else:
        assert 4 * m == SUBLANES
        upper = lax.broadcasted_iota(I32, (SUBLANES, bh.shape[1]), 0) < 2 * m
        for g in range(c // SUBLANES):
            r = g * SUBLANES + off
            ref = jnp.where(upper, b_scr[r:r + 1, sl], b_scr[r + 2 * m:r + 2 * m + 1, sl])
            pieces.append(-jnp.abs(bh[g * SUBLANES:(g + 1) * SUBLANES, :] - ref))
    return jnp.concatenate(pieces, axis=0)


def _hgrn_chunk(q_ref, z_ref, v_ref, lb_ref, tri_ref, mask_ref, o_ref, st_scr, f_scr, b_scr, rev):
    c = q_ref.shape[0]
    lb = lb_ref[...]
    f = lb + (1.0 - lb) * jax.nn.sigmoid(z_ref[...])
    f_scr[...] = f
    lf = jnp.log(f) * LOG2E
    lf_hi = lf.astype(BF16)
    lf_lo = (lf - lf_hi.astype(F32)).astype(BF16)
    tri = tri_ref[...]
    b_scr[...] = (jnp.dot(tri, lf_hi, preferred_element_type=F32)
                  + jnp.dot(tri, lf_lo, preferred_element_type=F32))
    last = 0 if rev else c - 1
    levels = _hgrn_levels(c)

    for h in range(q_ref.shape[1] // HG_D):
        sl = slice(h * HG_D, (h + 1) * HG_D)
        qh = q_ref[:, sl]
        vh = v_ref[:, sl]
        fh = f_scr[:, sl]
        kh = 1.0 - fh
        bh = b_scr[:, sl]
        b_last = b_scr[last:last + 1, sl]
        st = st_scr[h]
        vb = vh.astype(BF16)

        o = lax.dot_general((qh * jnp.exp2(bh)).astype(BF16), st.astype(BF16), _NT,
                            preferred_element_type=F32)
        scores = jnp.zeros((c, c), F32)
        for li, m in enumerate(levels):
            if m == 1:
                qm, km = qh * fh, kh
            else:
                e = jnp.exp2(_hgrn_neg_dist(bh, b_scr, sl, m, rev))
                qm, km = qh * e, kh * e
            sc = lax.dot_general(qm.astype(BF16), km.astype(BF16), _NT, preferred_element_type=F32)
            scores = scores + sc * mask_ref[li]
        o = o + jnp.dot(scores.astype(BF16), vb, preferred_element_type=F32)
        o = o + jnp.sum(qh * kh, axis=-1, keepdims=True) * vh
        o_ref[:, sl] = o

        ke = (kh * jnp.exp2(b_last - bh)).astype(BF16)
        st_scr[h] = st * jnp.exp2(b_last) + lax.dot_general(vb, ke, _TN, preferred_element_type=F32)


def _hgrn_kernel(*refs):
    ins, outs, (st_scr, f_scr, b_scr) = refs[:12], refs[12:14], refs[14:]

    @pl.when(pl.program_id(1) == 0)
    def _():
        st_scr[...] = jnp.zeros_like(st_scr)

    for dr in range(2):
        _hgrn_chunk(*ins[6 * dr:6 * dr + 6], outs[dr], st_scr.at[dr], f_scr.at[dr], b_scr.at[dr],
                    rev=dr == 1)


def _hgrn_scan(px, lb, batch, ctx_len, seq_len, col0):
    rows = px.shape[0]
    c = HG_CHUNK
    w = HG_HEADS * HG_D
    cb = col0 // w
    ncx, nl = ctx_len // c, seq_len // c
    lat0 = batch * ncx

    def row_block(rev, b, n):
        nc = ncx - 1 - n if rev else n
        nx = nl - 1 - (n - ncx) if rev else n - ncx
        return jnp.where(n < ncx, b * ncx + nc, lat0 + b * nl + nx)

    operands, in_specs, out_specs = [], [], []
    for rev in (False, True):
        rb = functools.partial(row_block, rev)
        tri, masks = _hgrn_constants(c, rev)
        zcol = cb + (2 if rev else 1)
        dr = int(rev)
        operands += [px, px, px, lb.reshape(2, 1, w), tri, masks]
        in_specs += [pl.BlockSpec((c, w), lambda b, n, rb=rb: (rb(b, n), cb)),
                     pl.BlockSpec((c, w), lambda b, n, rb=rb, zcol=zcol: (rb(b, n), zcol)),
                     pl.BlockSpec((c, w), lambda b, n, rb=rb: (rb(b, n), cb + 3)),
                     pl.BlockSpec((None, 1, w), lambda b, n, dr=dr: (dr, 0, 0)),
                     pl.BlockSpec((c, c), lambda b, n: (0, 0)),
                     pl.BlockSpec(masks.shape, lambda b, n: (0, 0, 0))]
        out_specs.append(pl.BlockSpec((c, w), lambda b, n, rb=rb: (rb(b, n), 0)))
    return pl.pallas_call(
        _hgrn_kernel,
        out_shape=(jax.ShapeDtypeStruct((rows, w), F32),) * 2,
        grid=(batch, ncx + nl),
        in_specs=in_specs,
        out_specs=tuple(out_specs),
        scratch_shapes=[pltpu.VMEM((2, HG_HEADS, HG_D, HG_D), F32),
                        pltpu.VMEM((2, c, w), F32),
                        pltpu.VMEM((2, c, w), F32)],
        compiler_params=_cparams(("parallel", "arbitrary")),
        name="hgrn_scan",
    )(*operands)


def _out_kernel(x_ref, mod_ref, conv_ref, of_ref, ob_ref, g_ref, ng_ref, w_ref, y_ref):
    o = of_ref[...] + ob_ref[...]
    parts = []
    for h in range(HG_HEADS):
        oh = o[:, h * HG_D:(h + 1) * HG_D]
        parts.append(oh * lax.rsqrt(jnp.mean(oh * oh, axis=-1, keepdims=True) + EPS))
    g = g_ref[...]
    rec = jnp.concatenate(parts, axis=-1) * ng_ref[...] * (g * jax.nn.sigmoid(g))
    ch = conv_ref.shape[1]
    acc = jnp.dot(conv_ref[...], w_ref[0, 0:ch, :], preferred_element_type=F32)
    acc = acc + jnp.dot(rec.astype(BF16), w_ref[0, ch:, :], preferred_element_type=F32)
    y_ref[...] = x_ref[...] + mod_ref[0, 2:3, :] * acc


def _out_proj(tok, mod, conv, o_f, o_b, px, norm_g, w, l, layout, skip_rows):
    rows, d = tok.shape
    ch = conv.shape[1]
    wr = o_f.shape[1]
    gcol = px.shape[1] // wr - 1
    tm = OUT_ROW_TILE
    off = skip_rows // tm
    assert conv.shape[0] == rows - skip_rows
    return pl.pallas_call(
        _out_kernel,
        out_shape=jax.ShapeDtypeStruct((rows - skip_rows, d), F32),
        grid=(rows // tm - off,),
        in_specs=[pl.BlockSpec((tm, d), lambda i: (i + off, 0)),
                  pl.BlockSpec((1, 6, d), lambda i: (_mod_row(i + off, tm, layout), 0, 0)),
                  pl.BlockSpec((tm, ch), lambda i: (i, 0)),
                  pl.BlockSpec((tm, wr), lambda i: (i + off, 0)),
                  pl.BlockSpec((tm, wr), lambda i: (i + off, 0)),
                  pl.BlockSpec((tm, wr), lambda i: (i + off, gcol)),
                  pl.BlockSpec((1, wr), lambda i: (0, 0)),
                  pl.BlockSpec((1, ch + wr, d), lambda i: (l, 0, 0))],
        out_specs=pl.BlockSpec((tm, d), lambda i: (i, 0)),
        compiler_params=_cparams(("parallel",)),
        name="out_proj",
    )(tok, mod, conv, o_f, o_b, px, norm_g.reshape(1, wr), w)


def _ffn_kernel(xa_ref, xb_ref, ma_ref, mb_ref, g_ref, w1_ref, w3_ref, w2_ref, y_ref, h_scr, *, n_half):
    f = pl.program_id(1)
    last_f = pl.num_programs(1) - 1
    tm = xa_ref.shape[0]
    pairs = ((xa_ref, ma_ref), (xb_ref, mb_ref))

    def run(halves):
        rows = halves * tm

        @pl.when(f == 0)
        def _():
            for k, (x_ref, m_ref) in enumerate(pairs[:halves]):
                h = _norm_mod(x_ref[...], g_ref[...], m_ref[0, 3:4, :], m_ref[0, 4:5, :])
                h_scr[k * tm:(k + 1) * tm, :] = h.astype(BF16)
            y_ref[0:rows, :] = jnp.zeros((rows, y_ref.shape[1]), F32)

        h = h_scr[0:rows, :]
        a = jnp.dot(h, w1_ref[...], preferred_element_type=F32)
        b = jnp.dot(h, w3_ref[...], preferred_element_type=F32)
        u = (a * jax.nn.sigmoid(a) * b).astype(BF16)
        y_ref[0:rows, :] += jnp.dot(u, w2_ref[...], preferred_element_type=F32)

        @pl.when(f == last_f)
        def _():
            for k, (x_ref, m_ref) in enumerate(pairs[:halves]):
                sl = slice(k * tm, (k + 1) * tm)
                y_ref[sl, :] = x_ref[...] + m_ref[0, 5:6, :] * y_ref[sl, :]

    _pair_dispatch(n_half, run)


def _dense_ffn(tok, mod, g, w1, w3, w2, layout, tf=256):
    rows, d = tok.shape
    ff = w1.shape[1]
    n_half = rows // ROW_TILE
    index = lambda fn: (lambda i, f: fn(i))
    return pl.pallas_call(
        functools.partial(_ffn_kernel, n_half=n_half),
        out_shape=jax.ShapeDtypeStruct((rows, d), F32),
        grid=(pl.cdiv(n_half, 2), ff // tf),
        in_specs=_pair_specs(d, layout, n_half, index) + [
            pl.BlockSpec((1, d), lambda i, f: (0, 0)),
            pl.BlockSpec((d, tf), lambda i, f: (0, f)),
            pl.BlockSpec((d, tf), lambda i, f: (0, f)),
            pl.BlockSpec((tf, d), lambda i, f: (f, 0))],
        out_specs=pl.BlockSpec((2 * ROW_TILE, d), lambda i, f: (i, 0)),
        scratch_shapes=[pltpu.VMEM((2 * ROW_TILE, d), BF16)],
        compiler_params=_cparams(("parallel", "arbitrary")),
        name="dense_ffn",
    )(tok, tok, mod, mod, g, w1, w3, w2)


def _router_kernel(x_ref, mod_ref, g_ref, wr_ref, h_ref, lg_ref):
    h = _norm_mod(x_ref[...], g_ref[...], mod_ref[0, 3:4, :], mod_ref[0, 4:5, :])
    h_ref[...] = h
    lg_ref[...] = jnp.dot(h, wr_ref[...], preferred_element_type=F32,
                          precision=lax.Precision.HIGHEST)


def _router(tok, mod, g, wr_pad, layout, skip_rows):
    rows, d = tok.shape
    tm = ROW_TILE
    off = skip_rows // tm
    nrow = rows - skip_rows
    ne = wr_pad.shape[1]
    return pl.pallas_call(
        _router_kernel,
        out_shape=(jax.ShapeDtypeStruct((nrow, d), F32), jax.ShapeDtypeStruct((nrow, ne), F32)),
        grid=(nrow // tm,),
        in_specs=[pl.BlockSpec((tm, d), lambda i: (i + off, 0)),
                  pl.BlockSpec((1, 6, d), lambda i: (_mod_row(i + off, tm, layout), 0, 0)),
                  pl.BlockSpec((1, d), lambda i: (0, 0)),
                  pl.BlockSpec((d, ne), lambda i: (0, 0))],
        out_specs=(pl.BlockSpec((tm, d), lambda i: (i, 0)),
                   pl.BlockSpec((tm, ne), lambda i: (i, 0))),
        compiler_params=_cparams(("parallel",)),
        name="moe_router",
    )(tok, mod, g, wr_pad)


def _row_copy(src_hbm, src_row, dst_vmem, dst_row, sem):
    return pltpu.make_async_copy(src_hbm.at[pl.ds(src_row, 1), :], dst_vmem.at[pl.ds(dst_row, 1), :], sem)


def _gather_start(idx_ref, idx_base, idx_stride, src_hbm, dst_vmem, nrows, sem, priorities=1):
    assert nrows % DMA_UNROLL == 0

    def issue(i, carry):
        for u in range(DMA_UNROLL):
            r = i * DMA_UNROLL + u
            _row_copy(src_hbm, idx_ref[idx_base + r * idx_stride], dst_vmem, r, sem).start(
                priority=u % priorities)
        return carry

    lax.fori_loop(0, nrows // DMA_UNROLL, issue, 0)


def _gather_wait(src_hbm, dst_vmem, nrows, sem):
    pltpu.make_async_copy(src_hbm.at[pl.ds(0, nrows), :], dst_vmem.at[pl.ds(0, nrows), :], sem).wait()


def _experts_kernel(blk_e_ref, nsub_ref, nb_ref, slot_ref, h_hbm, w1_ref, w3_ref, w2_ref, y_ref,
                    xb_scr, sem):
    g = pl.program_id(0)
    f = pl.program_id(1)
    used = g < nb_ref[0]

    @pl.when((f == 0) & jnp.logical_not(used))
    def _():
        y_ref[...] = jnp.zeros_like(y_ref)

    @pl.when(used)
    def _():
        for nsub in range(1, MOE_ROWS // MOE_SUB + 1):
            nrows = nsub * MOE_SUB

            @pl.when(nsub_ref[g] == nsub)
            def _():
                @pl.when(f == 0)
                def _():
                    _gather_start(slot_ref, g * MOE_ROWS, 1, h_hbm, y_ref, nrows, sem)
                    _gather_wait(h_hbm, y_ref, nrows, sem)
                    xb_scr[0:nrows, :] = y_ref[0:nrows, :].astype(BF16)
                    y_ref[...] = jnp.zeros_like(y_ref)

                x = xb_scr[0:nrows, :]
                a = lax.dot_general(x, w1_ref[0], _NN, preferred_element_type=F32)
                b = lax.dot_general(x, w3_ref[0], _NN, preferred_element_type=F32)
                u = (a * jax.nn.sigmoid(a) * b).astype(BF16)
                y_ref[0:nrows, :] += lax.dot_general(u, w2_ref[0], _NN, preferred_element_type=F32)


def _experts(h, slot_tok, blk_e, nsub, nb_used, w1, w3, w2, tf=512):
    n, d = h.shape
    ff = w1.shape[2]
    nf = ff // tf
    n_blocks = slot_tok.shape[0] // MOE_ROWS

    def gidx(g, nb):
        return jnp.minimum(g, nb[0] - 1)

    def fidx(g, f, nb):
        return jnp.where(g < nb[0], f, nf - 1)

    grid_spec = pltpu.PrefetchScalarGridSpec(
        num_scalar_prefetch=4,
        grid=(n_blocks, nf),
        in_specs=[pl.BlockSpec(memory_space=pl.ANY),
                  pl.BlockSpec((1, d, tf), lambda g, f, be, ns, nb, st: (be[gidx(g, nb)], 0, fidx(g, f, nb))),
                  pl.BlockSpec((1, d, tf), lambda g, f, be, ns, nb, st: (be[gidx(g, nb)], 0, fidx(g, f, nb))),
                  pl.BlockSpec((1, tf, d), lambda g, f, be, ns, nb, st: (be[gidx(g, nb)], fidx(g, f, nb), 0))],
        out_specs=pl.BlockSpec((MOE_ROWS, d), lambda g, f, be, ns, nb, st: (g, 0)),
        scratch_shapes=[pltpu.VMEM((MOE_ROWS, d), BF16),
                        pltpu.SemaphoreType.DMA],
    )
    return pl.pallas_call(
        _experts_kernel,
        out_shape=jax.ShapeDtypeStruct((n_blocks * MOE_ROWS, d), F32),
        grid_spec=grid_spec,
        compiler_params=_cparams(("arbitrary", "arbitrary")),
        name="moe_experts",
    )(blk_e, nsub, nb_used, slot_tok, h, w1, w3, w2)


def _combine_kernel(dest_ref, x_ref, mod_ref, w_ref, gf_ref, y_hbm, o_ref, y0_scr, y1_scr, sem):
    tm = x_ref.shape[0]
    base = pl.program_id(0) * tm * TOP_K
    _gather_start(dest_ref, base, TOP_K, y_hbm, y0_scr, tm, sem.at[0], priorities=2)
    _gather_start(dest_ref, base + 1, TOP_K, y_hbm, y1_scr, tm, sem.at[1], priorities=2)
    _gather_wait(y_hbm, y0_scr, tm, sem.at[0])
    _gather_wait(y_hbm, y1_scr, tm, sem.at[1])
    w = w_ref[...]
    y = w[:, 0:1] * y0_scr[...] + w[:, 1:2] * y1_scr[...]
    x = x_ref[...] + mod_ref[0, 5:6, :] * y
    o_ref[...] = x * lax.rsqrt(jnp.mean(x * x, axis=-1, keepdims=True) + EPS) * gf_ref[...]


def _combine_final(tok, mod, y, dest, top_w, g_final, layout, skip_rows):
    rows, d = tok.shape
    tm = ROW_TILE
    off = skip_rows // tm
    nrow = rows - skip_rows
    grid_spec = pltpu.PrefetchScalarGridSpec(
        num_scalar_prefetch=1,
        grid=(nrow // tm,),
        in_specs=[pl.BlockSpec((tm, d), lambda i, ds: (i + off, 0)),
                  pl.BlockSpec((1, 6, d), lambda i, ds: (_mod_row(i + off, tm, layout), 0, 0)),
                  pl.BlockSpec((tm, TOP_K), lambda i, ds: (i, 0)),
                  pl.BlockSpec((1, d), lambda i, ds: (0, 0)),
                  pl.BlockSpec(memory_space=pl.ANY)],
        out_specs=pl.BlockSpec((tm, d), lambda i, ds: (i, 0)),
        scratch_shapes=[pltpu.VMEM((tm, d), F32), pltpu.VMEM((tm, d), F32),
                        pltpu.SemaphoreType.DMA((TOP_K,))],
    )
    return pl.pallas_call(
        _combine_kernel,
        out_shape=jax.ShapeDtypeStruct((nrow, d), F32),
        grid_spec=grid_spec,
        compiler_params=_cparams(("arbitrary",)),
        name="moe_combine_final",
    )(dest, tok, mod, top_w, g_final.reshape(1, d), y)


def _route(logits):
    n = logits.shape[0]
    na = n * TOP_K
    top_logit, top_e = lax.top_k(logits, TOP_K)
    top_w = jax.nn.softmax(top_logit, axis=-1)
    a_e = top_e.reshape(-1).astype(I32)
    order = jnp.argsort(a_e, stable=True).astype(I32)
    inv = jnp.argsort(order).astype(I32)
    counts = jnp.sum(a_e[:, None] == jnp.arange(N_EXPERTS, dtype=I32)[None, :], axis=0).astype(I32)
    padded = (counts + MOE_ROWS - 1) // MOE_ROWS * MOE_ROWS
    pad_end = jnp.cumsum(padded)
    pad_start = pad_end - padded
    start = jnp.cumsum(counts) - counts
    dest = inv + (pad_start - start)[a_e]

    n_blocks = na // MOE_ROWS + N_EXPERTS
    slot = jnp.arange(n_blocks * MOE_ROWS, dtype=I32)
    slot_e = jnp.minimum(jnp.sum(slot[:, None] >= pad_end[None, :], axis=1), N_EXPERTS - 1).astype(I32)
    rank = slot - pad_start[slot_e]
    src = jnp.clip(start[slot_e] + rank, 0, na - 1)
    slot_tok = jnp.where(rank < counts[slot_e], order[src] // TOP_K, slot % n)

    blk0 = jnp.arange(n_blocks, dtype=I32) * MOE_ROWS
    blk_e = slot_e[::MOE_ROWS]
    blk_rows = jnp.clip(counts[blk_e] - (blk0 - pad_start[blk_e]), 0, MOE_ROWS)
    nb_used = (pad_end[-1:] // MOE_ROWS).astype(I32)
    nsub = jnp.where(blk0 < pad_end[-1], (blk_rows + MOE_SUB - 1) // MOE_SUB, 0).astype(I32)
    return top_w, slot_tok, dest, blk_e, nsub, nb_used


def kernel(x, c, ctx, c_ctx, w_ada, b_ada, g_mix, w_in, conv_w, conv_b, conv_ln_g, conv_ln_b, hgrn_lb_logits, hgrn_norm_g, w_out, g_ffn, ffn_w1, ffn_w3, ffn_w2, router_w, moe_w1, moe_w3, moe_w2, g_final):
    batch, seq, d = x.shape
    ctx_len = ctx.shape[1]
    depth = w_ada.shape[0]
    assert depth == 2 and batch == 2
    conv_ch = conv_w.shape[2]
    ctx_rows = batch * ctx_len
    layout = (ctx_rows, seq)

    cs = jnp.cumsum(jax.nn.softmax(hgrn_lb_logits.astype(F32), axis=1), axis=1)
    lb_all = cs - cs[:, :1]

    cc = jnp.zeros((MOD_ROWS, d), F32).at[0:batch].set(c).at[batch].set(c_ctx)
    mod_all = _adaln(cc, w_ada, b_ada).reshape(depth, MOD_ROWS, 6, d)

    tok = jnp.concatenate([ctx.reshape(ctx_rows, d), x.reshape(batch * seq, d)], axis=0)
    row = lambda a: a.reshape(1, -1)
    w_in_b = w_in.astype(BF16)
    w_out_b = w_out.astype(BF16)

    for l in range(depth):
        last = l == depth - 1
        mod = mod_all[l]
        px = _in_proj(tok, mod, row(g_mix[l]), w_in_b, l, layout)
        o_f, o_b = _hgrn_scan(px, lb_all[:, l], batch, ctx_len, seq, 2 * conv_ch)
        conv = _conformer_conv(px, conv_w[l], conv_b[l], conv_ln_g[l], conv_ln_b[l], ctx_len,
                               with_ctx=not last, skip_blocks=ctx_rows // ctx_len)
        skip_rows = ctx_rows if last else 0
        tok = _out_proj(tok, mod, conv, o_f, o_b, px, hgrn_norm_g[l], w_out_b, l, layout, skip_rows)
        j = l // 2
        if l % 2 == 0:
            tok = _dense_ffn(tok, mod, row(g_ffn[l]), ffn_w1[j].astype(BF16), ffn_w3[j].astype(BF16),
                             ffn_w2[j].astype(BF16), layout)
        else:
            assert last
            wr_pad = jnp.zeros((d, 128), F32).at[:, :N_EXPERTS].set(router_w[j])
            lat_layout = (0, seq)
            h, logits = _router(tok, mod, row(g_ffn[l]), wr_pad, lat_layout, 0)
            top_w, slot_tok, dest, blk_e, nsub, nb_used = _route(logits[:, :N_EXPERTS])
            y = _experts(h, slot_tok, blk_e, nsub, nb_used, moe_w1[j], moe_w3[j], moe_w2[j])
            out = _combine_final(tok, mod, y, dest, top_w, g_final, lat_layout, 0)
    return out.reshape(batch, seq, d)
```

```python
import functools
import math

import numpy as np
import jax
import jax.numpy as jnp
from jax import lax
from jax.experimental import pallas as pl
from jax.experimental.pallas import tpu as pltpu

F32 = jnp.float32
BF16 = jnp.bfloat16
I32 = jnp.int32
EPS = 1e-6
LOG2E = math.log2(math.e)

SUBLANES = 8
GRID_W = 64
CONV_K = 31
CONV_PAD = 16
HG_HEADS = 8
HG_D = 128
HG_CHUNK = 128
N_EXPERTS = 8
TOP_K = 2
MOE_SUB = 256
MOE_ROWS = 4 * MOE_SUB
ROW_TILE = 512
OUT_ROW_TILE = 512
MOD_ROWS = 8
DMA_UNROLL = 8
V7X_VMEM_LIMIT = 56 * 1024 * 1024

_NN = (((1,), (0,)), ((), ()))
_NT = (((1,), (1,)), ((), ()))
_TN = (((0,), (0,)), ((), ()))


def _cparams(sem):
    return pltpu.CompilerParams(dimension_semantics=sem, vmem_limit_bytes=V7X_VMEM_LIMIT)


def _mod_row(i, tm, layout):
    ctx_rows, seq = layout
    assert ctx_rows % tm == 0 and seq % tm == 0
    return jnp.where(i < ctx_rows // tm, 2, (i - ctx_rows // tm) // (seq // tm))


def _norm_mod(x, g, shift, scale):
    y = x * lax.rsqrt(jnp.mean(x * x, axis=-1, keepdims=True) + EPS) * g
    return y * (1.0 + scale) + shift


def _ada_kernel(c_ref, w_ref, b_ref, o_ref):
    c = c_ref[...]
    a = c * jax.nn.sigmoid(c)
    o_ref[0] = jnp.dot(a, w_ref[0], preferred_element_type=F32) + b_ref[0]


def _adaln(cc, w_ada, b_ada, tn=1024):
    depth, d, n = w_ada.shape
    return pl.pallas_call(
        _ada_kernel,
        out_shape=jax.ShapeDtypeStruct((depth, MOD_ROWS, n), F32),
        grid=(depth, n // tn),
        in_specs=[pl.BlockSpec((MOD_ROWS, d), lambda l, j: (0, 0)),
                  pl.BlockSpec((1, d, tn), lambda l, j: (l, 0, j)),
                  pl.BlockSpec((1, 1, tn), lambda l, j: (l, 0, j))],
        out_specs=pl.BlockSpec((1, MOD_ROWS, tn), lambda l, j: (l, 0, j)),
        compiler_params=_cparams(("parallel", "parallel")),
        name="adaln",
    )(cc, w_ada, b_ada.reshape(depth, 1, n))


def _pair_dispatch(n_half, run):
    i = pl.program_id(0)
    if n_half % 2:
        pl.when(i < n_half // 2)(lambda: run(2))
        pl.when(i == n_half // 2)(lambda: run(1))
    else:
        run(2)


def _pair_specs(d, layout, n_half, index):
    halves = [lambda i: 2 * i, lambda i: jnp.minimum(2 * i + 1, n_half - 1)]
    rows = [pl.BlockSpec((ROW_TILE, d), index(lambda i, h=h: (h(i), 0))) for h in halves]
    mods = [pl.BlockSpec((1, 6, d), index(lambda i, h=h: (_mod_row(h(i), ROW_TILE, layout), 0, 0)))
            for h in halves]
    return rows + mods


def _proj_kernel(xa_ref, xb_ref, ma_ref, mb_ref, g_ref, w_ref, o_ref, h_scr, *, n_half):
    tm = xa_ref.shape[0]
    first_col = pl.program_id(1) == 0

    def run(halves):
        rows = halves * tm

        @pl.when(first_col)
        def _():
            for k, (x_ref, m_ref) in enumerate(((xa_ref, ma_ref), (xb_ref, mb_ref))[:halves]):
                h = _norm_mod(x_ref[...], g_ref[...], m_ref[0, 0:1, :], m_ref[0, 1:2, :])
                h_scr[k * tm:(k + 1) * tm, :] = h.astype(BF16)

        o_ref[0:rows, :] = jnp.dot(h_scr[0:rows, :], w_ref[0], preferred_element_type=F32)

    _pair_dispatch(n_half, run)


def _in_proj(tok, mod, g, w, l, layout, tn=1024):
    rows, d = tok.shape
    n = w.shape[2]
    n_half = rows // ROW_TILE
    index = lambda fn: (lambda i, j: fn(i))
    return pl.pallas_call(
        functools.partial(_proj_kernel, n_half=n_half),
        out_shape=jax.ShapeDtypeStruct((rows, n), F32),
        grid=(pl.cdiv(n_half, 2), n // tn),
        in_specs=_pair_specs(d, layout, n_half, index) + [
            pl.BlockSpec((1, d), lambda i, j: (0, 0)),
            pl.BlockSpec((1, d, tn), lambda i, j: (l, 0, j))],
        out_specs=pl.BlockSpec((2 * ROW_TILE, tn), lambda i, j: (i, j)),
        scratch_shapes=[pltpu.VMEM((2 * ROW_TILE, d), BF16)],
        compiler_params=_cparams(("parallel", "arbitrary")),
        name="in_proj",
    )(tok, tok, mod, mod, g, w)


def _conv_segment(u, w_ref, pad_scr, shift_scr, seg_len):
    ch = u.shape[1]
    zeros = jnp.zeros((CONV_PAD, ch), F32)
    pad_scr[0:CONV_PAD, :] = zeros
    pad_scr[CONV_PAD + seg_len:2 * CONV_PAD + seg_len, :] = zeros
    pad_scr[CONV_PAD:CONV_PAD + seg_len, :] = u
    base = CONV_PAD - CONV_K // 2
    span = seg_len + (CONV_K - 1) // SUBLANES * SUBLANES
    acc = jnp.zeros((seg_len, ch), F32)
    for r in range(SUBLANES):
        shift_scr[0:span, :] = pad_scr[base + r:base + r + span, :]
        for k in range(r, CONV_K, SUBLANES):
            acc = acc + shift_scr[k - r:k - r + seg_len, :] * w_ref[k:k + 1, :]
    return acc


def _conv_kernel(val_ref, gate_ref, w_ref, b_ref, lg_ref, lb_ref, o_ref, pad_scr, shift_scr, *,
                 ctx_blocks, ctx_len):
    rows = val_ref.shape[0]

    def run(seg_len):
        for s in range(rows // seg_len):
            sl = slice(s * seg_len, (s + 1) * seg_len)
            u = val_ref[sl, :] * jax.nn.sigmoid(gate_ref[sl, :])
            y = _conv_segment(u, w_ref, pad_scr, shift_scr, seg_len) + b_ref[...]
            mu = jnp.mean(y, axis=-1, keepdims=True)
            yc = y - mu
            var = jnp.mean(yc * yc, axis=-1, keepdims=True)
            z = yc * lax.rsqrt(var + EPS) * lg_ref[...] + lb_ref[...]
            o_ref[sl, :] = (z * jax.nn.sigmoid(z)).astype(o_ref.dtype)

    if ctx_blocks:
        is_ctx = pl.program_id(0) < ctx_blocks
        pl.when(is_ctx)(lambda: run(ctx_len))
        pl.when(jnp.logical_not(is_ctx))(lambda: run(GRID_W))
    else:
        run(GRID_W)


def _conformer_conv(px, w, b, lg, lb, ctx_len, with_ctx, skip_blocks):
    rows = px.shape[0]
    ch = w.shape[1]
    tm = ctx_len
    off = 0 if with_ctx else skip_blocks
    kern = functools.partial(_conv_kernel, ctx_blocks=skip_blocks if with_ctx else 0, ctx_len=ctx_len)
    vec = lambda a: a.reshape(1, ch)
    return pl.pallas_call(
        kern,
        out_shape=jax.ShapeDtypeStruct((rows - off * tm, ch), BF16),
        grid=(rows // tm - off,),
        in_specs=[pl.BlockSpec((tm, ch), lambda i: (i + off, 0)),
                  pl.BlockSpec((tm, ch), lambda i: (i + off, 1)),
                  pl.BlockSpec((CONV_K, ch), lambda i: (0, 0)),
                  pl.BlockSpec((1, ch), lambda i: (0, 0)),
                  pl.BlockSpec((1, ch), lambda i: (0, 0)),
                  pl.BlockSpec((1, ch), lambda i: (0, 0))],
        out_specs=pl.BlockSpec((tm, ch), lambda i: (i, 0)),
        scratch_shapes=[pltpu.VMEM((tm + 2 * CONV_PAD, ch), F32),
                        pltpu.VMEM((tm + 2 * CONV_PAD, ch), F32)],
        compiler_params=_cparams(("parallel",)),
        name="conformer_conv",
    )(px, px, w, vec(b), vec(lg), vec(lb))


def _hgrn_levels(c):
    return [c >> (i + 1) for i in range(c.bit_length() - 1)]


def _hgrn_constants(c, rev):
    p = np.arange(c)[::-1] if rev else np.arange(c)
    pt, ps = p[:, None], p[None, :]
    tri = (ps <= pt).astype(np.float32)
    masks = [((pt // (2 * m)) == (ps // (2 * m))) & ((pt // m) % 2 == 1) & ((ps // m) % 2 == 0)
             for m in _hgrn_levels(c)]
    return jnp.asarray(tri, dtype=BF16), jnp.asarray(np.stack(masks).astype(np.float32))


def _hgrn_neg_dist(bh, b_scr, sl, m, rev):
    c = bh.shape[0]
    off = m if rev else m - 1
    pieces = []
    if m >= SUBLANES:
        for j in range(c // (2 * m)):
            lo, mid, hi = j * 2 * m, j * 2 * m + m, (j + 1) * 2 * m
            ref = b_scr[lo + off:lo + off + 1, sl]
            first, second = (bh[lo:mid, :] - ref, ref - bh[mid:hi, :]) if rev else \
                            (ref - bh[lo:mid, :], bh[mid:hi, :] - ref)
            pieces += [first, second]
    elif 2 * m == SUBLANES:
        for g in range(c // SUBLANES):
            r = g * SUBLANES + off
            pieces.append(-jnp.abs(bh[g * SUBLANES:(g + 1) * SUBLANES, :] - b_scr[r:r + 1, sl]))
    else:
        assert 4 * m == SUBLANES
        upper = lax.broadcasted_iota(I32, (SUBLANES, bh.shape[1]), 0) < 2 * m
        for g in range(c // SUBLANES):
            r = g * SUBLANES + off
            ref = jnp.where(upper, b_scr[r:r + 1, sl], b_scr[r + 2 * m:r + 2 * m + 1, sl])
            pieces.append(-jnp.abs(bh[g * SUBLANES:(g + 1) * SUBLANES, :] - ref))
    return jnp.concatenate(pieces, axis=0)


def _hgrn_chunk(q_ref, z_ref, v_ref, lb_ref, tri_ref, mask_ref, o_ref, st_scr, f_scr, b_scr, rev):
    c = q_ref.shape[0]
    lb = lb_ref[...]
    f = lb + (1.0 - lb) * jax.nn.sigmoid(z_ref[...])
    f_scr[...] = f
    lf = jnp.log(f) * LOG2E
    lf_hi = lf.astype(BF16)
    lf_lo = (lf - lf_hi.astype(F32)).astype(BF16)
    tri = tri_ref[...]
    b_scr[...] = (jnp.dot(tri, lf_hi, preferred_element_type=F32)
                  + jnp.dot(tri, lf_lo, preferred_element_type=F32))
    last = 0 if rev else c - 1
    levels = _hgrn_levels(c)

    for h in range(q_ref.shape[1] // HG_D):
        sl = slice(h * HG_D, (h + 1) * HG_D)
        qh = q_ref[:, sl]
        vh = v_ref[:, sl]
        fh = f_scr[:, sl]
        kh = 1.0 - fh
        bh = b_scr[:, sl]
        b_last = b_scr[last:last + 1, sl]
        st = st_scr[h]
        vb = vh.astype(BF16)

        o = lax.dot_general((qh * jnp.exp2(bh)).astype(BF16), st.astype(BF16), _NT,
                            preferred_element_type=F32)
        scores = jnp.zeros((c, c), F32)
        for li, m in enumerate(levels):
            if m == 1:
                qm, km = qh * fh, kh
            else:
                e = jnp.exp2(_hgrn_neg_dist(bh, b_scr, sl, m, rev))
                qm, km = qh * e, kh * e
            sc = lax.dot_general(qm.astype(BF16), km.astype(BF16), _NT, preferred_element_type=F32)
            scores = scores + sc * mask_ref[li]
        o = o + jnp.dot(scores.astype(BF16), vb, preferred_element_type=F32)
        o = o + jnp.sum(qh * kh, axis=-1, keepdims=True) * vh
        o_ref[:, sl] = o

        ke = (kh * jnp.exp2(b_last - bh)).astype(BF16)
        st_scr[h] = st * jnp.exp2(b_last) + lax.dot_general(vb, ke, _TN, preferred_element_type=F32)


def _hgrn_kernel(*refs):
    ins, outs, (st_scr, f_scr, b_scr) = refs[:12], refs[12:14], refs[14:]

    @pl.when(pl.program_id(1) == 0)
    def _():
        st_scr[...] = jnp.zeros_like(st_scr)

    for dr in range(2):
        _hgrn_chunk(*ins[6 * dr:6 * dr + 6], outs[dr], st_scr.at[dr], f_scr.at[dr], b_scr.at[dr],
                    rev=dr == 1)


def _hgrn_scan(px, lb, batch, ctx_len, seq_len, col0):
    rows = px.shape[0]
    c = HG_CHUNK
    w = HG_HEADS * HG_D
    cb = col0 // w
    ncx, nl = ctx_len // c, seq_len // c
    lat0 = batch * ncx

    def row_block(rev, b, n):
        nc = ncx - 1 - n if rev else n
        nx = nl - 1 - (n - ncx) if rev else n - ncx
        return jnp.where(n < ncx, b * ncx + nc, lat0 + b * nl + nx)

    operands, in_specs, out_specs = [], [], []
    for rev in (False, True):
        rb = functools.partial(row_block, rev)
        tri, masks = _hgrn_constants(c, rev)
        zcol = cb + (2 if rev else 1)
        dr = int(rev)
        operands += [px, px, px, lb.reshape(2, 1, w), tri, masks]
        in_specs += [pl.BlockSpec((c, w), lambda b, n, rb=rb: (rb(b, n), cb)),
                     pl.BlockSpec((c, w), lambda b, n, rb=rb, zcol=zcol: (rb(b, n), zcol)),
                     pl.BlockSpec((c, w), lambda b, n, rb=rb: (rb(b, n), cb + 3)),
                     pl.BlockSpec((None, 1, w), lambda b, n, dr=dr: (dr, 0, 0)),
                     pl.BlockSpec((c, c), lambda b, n: (0, 0)),
                     pl.BlockSpec(masks.shape, lambda b, n: (0, 0, 0))]
        out_specs.append(pl.BlockSpec((c, w), lambda b, n, rb=rb: (rb(b, n), 0)))
    return pl.pallas_call(
        _hgrn_kernel,
        out_shape=(jax.ShapeDtypeStruct((rows, w), F32),) * 2,
        grid=(batch, ncx + nl),
        in_specs=in_specs,
        out_specs=tuple(out_specs),
        scratch_shapes=[pltpu.VMEM((2, HG_HEADS, HG_D, HG_D), F32),
                        pltpu.VMEM((2, c, w), F32),
                        pltpu.VMEM((2, c, w), F32)],
        compiler_params=_cparams(("parallel", "arbitrary")),
        name="hgrn_scan",
    )(*operands)


def _out_kernel(x_ref, mod_ref, conv_ref, of_ref, ob_ref, g_ref, ng_ref, w_ref, y_ref):
    o = of_ref[...] + ob_ref[...]
    parts = []
    for h in range(HG_HEADS):
        oh = o[:, h * HG_D:(h + 1) * HG_D]
        parts.append(oh * lax.rsqrt(jnp.mean(oh * oh, axis=-1, keepdims=True) + EPS))
    g = g_ref[...]
    rec = jnp.concatenate(parts, axis=-1) * ng_ref[...] * (g * jax.nn.sigmoid(g))
    ch = conv_ref.shape[1]
    acc = jnp.dot(conv_ref[...], w_ref[0, 0:ch, :], preferred_element_type=F32)
    acc = acc + jnp.dot(rec.astype(BF16), w_ref[0, ch:, :], preferred_element_type=F32)
    y_ref[...] = x_ref[...] + mod_ref[0, 2:3, :] * acc


def _out_proj(tok, mod, conv, o_f, o_b, px, norm_g, w, l, layout, skip_rows):
    rows, d = tok.shape
    ch = conv.shape[1]
    wr = o_f.shape[1]
    gcol = px.shape[1] // wr - 1
    tm = OUT_ROW_TILE
    off = skip_rows // tm
    assert conv.shape[0] == rows - skip_rows
    return pl.pallas_call(
        _out_kernel,
        out_shape=jax.ShapeDtypeStruct((rows - skip_rows, d), F32),
        grid=(rows // tm - off,),
        in_specs=[pl.BlockSpec((tm, d), lambda i: (i + off, 0)),
                  pl.BlockSpec((1, 6, d), lambda i: (_mod_row(i + off, tm, layout), 0, 0)),
                  pl.BlockSpec((tm, ch), lambda i: (i, 0)),
                  pl.BlockSpec((tm, wr), lambda i: (i + off, 0)),
                  pl.BlockSpec((tm, wr), lambda i: (i + off, 0)),
                  pl.BlockSpec((tm, wr), lambda i: (i + off, gcol)),
                  pl.BlockSpec((1, wr), lambda i: (0, 0)),
                  pl.BlockSpec((1, ch + wr, d), lambda i: (l, 0, 0))],
        out_specs=pl.BlockSpec((tm, d), lambda i: (i, 0)),
        compiler_params=_cparams(("parallel",)),
        name="out_proj",
    )(tok, mod, conv, o_f, o_b, px, norm_g.reshape(1, wr), w)


def _ffn_kernel(x_ref, mod_ref, g_ref, w1_ref, w3_ref, w2_ref, y_ref, h_scr, acc_scr):
    f = pl.program_id(1)

    @pl.when(f == 0)
    def _():
        h = _norm_mod(x_ref[...], g_ref[...], mod_ref[0, 3:4, :], mod_ref[0, 4:5, :])
        h_scr[...] = h.astype(BF16)
        acc_scr[...] = jnp.zeros_like(acc_scr)

    h = h_scr[...]
    a = jnp.dot(h, w1_ref[...], preferred_element_type=F32)
    b = jnp.dot(h, w3_ref[...], preferred_element_type=F32)
    u = (a * jax.nn.sigmoid(a) * b).astype(BF16)
    acc_scr[...] += jnp.dot(u, w2_ref[...], preferred_element_type=F32)

    @pl.when(f == pl.num_programs(1) - 1)
    def _():
        y_ref[...] = x_ref[...] + mod_ref[0, 5:6, :] * acc_scr[...]


def _dense_ffn(tok, mod, g, w1, w3, w2, layout, tf=512):
    rows, d = tok.shape
    ff = w1.shape[1]
    tm = ROW_TILE
    return pl.pallas_call(
        _ffn_kernel,
        out_shape=jax.ShapeDtypeStruct((rows, d), F32),
        grid=(rows // tm, ff // tf),
        in_specs=[pl.BlockSpec((tm, d), lambda i, f: (i, 0)),
                  pl.BlockSpec((1, 6, d), lambda i, f: (_mod_row(i, tm, layout), 0, 0)),
                  pl.BlockSpec((1, d), lambda i, f: (0, 0)),
                  pl.BlockSpec((d, tf), lambda i, f: (0, f)),
                  pl.BlockSpec((d, tf), lambda i, f: (0, f)),
                  pl.BlockSpec((tf, d), lambda i, f: (f, 0))],
        out_specs=pl.BlockSpec((tm, d), lambda i, f: (i, 0)),
        scratch_shapes=[pltpu.VMEM((tm, d), BF16), pltpu.VMEM((tm, d), F32)],
        compiler_params=_cparams(("parallel", "arbitrary")),
        name="dense_ffn",
    )(tok, mod, g, w1, w3, w2)


def _router_kernel(x_ref, mod_ref, g_ref, wr_ref, h_ref, lg_ref):
    h = _norm_mod(x_ref[...], g_ref[...], mod_ref[0, 3:4, :], mod_ref[0, 4:5, :])
    h_ref[...] = h
    lg_ref[...] = jnp.dot(h, wr_ref[...], preferred_element_type=F32,
                          precision=lax.Precision.HIGHEST)


def _router(tok, mod, g, wr_pad, layout, skip_rows):
    rows, d = tok.shape
    tm = ROW_TILE
    off = skip_rows // tm
    nrow = rows - skip_rows
    ne = wr_pad.shape[1]
    return pl.pallas_call(
        _router_kernel,
        out_shape=(jax.ShapeDtypeStruct((nrow, d), F32), jax.ShapeDtypeStruct((nrow, ne), F32)),
        grid=(nrow // tm,),
        in_specs=[pl.BlockSpec((tm, d), lambda i: (i + off, 0)),
                  pl.BlockSpec((1, 6, d), lambda i: (_mod_row(i + off, tm, layout), 0, 0)),
                  pl.BlockSpec((1, d), lambda i: (0, 0)),
                  pl.BlockSpec((d, ne), lambda i: (0, 0))],
        out_specs=(pl.BlockSpec((tm, d), lambda i: (i, 0)),
                   pl.BlockSpec((tm, ne), lambda i: (i, 0))),
        compiler_params=_cparams(("parallel",)),
        name="moe_router",
    )(tok, mod, g, wr_pad)


def _row_copy(src_hbm, src_row, dst_vmem, dst_row, sem):
    return pltpu.make_async_copy(src_hbm.at[pl.ds(src_row, 1), :], dst_vmem.at[pl.ds(dst_row, 1), :], sem)


def _gather_start(idx_ref, idx_base, idx_stride, src_hbm, dst_vmem, nrows, sem, priorities=1):
    assert nrows % DMA_UNROLL == 0

    def issue(i, carry):
        for u in range(DMA_UNROLL):
            r = i * DMA_UNROLL + u
            _row_copy(src_hbm, idx_ref[idx_base + r * idx_stride], dst_vmem, r, sem).start(
                priority=u % priorities)
        return carry

    lax.fori_loop(0, nrows // DMA_UNROLL, issue, 0)


def _gather_wait(src_hbm, dst_vmem, nrows, sem):
    pltpu.make_async_copy(src_hbm.at[pl.ds(0, nrows), :], dst_vmem.at[pl.ds(0, nrows), :], sem).wait()


def _experts_kernel(blk_e_ref, nsub_ref, nb_ref, slot_ref, h_hbm, w1_ref, w3_ref, w2_ref, y_ref,
                    xb_scr, sem):
    g = pl.program_id(0)
    f = pl.program_id(1)
    used = g < nb_ref[0]

    @pl.when((f == 0) & jnp.logical_not(used))
    def _():
        y_ref[...] = jnp.zeros_like(y_ref)

    @pl.when(used)
    def _():
        for nsub in range(1, MOE_ROWS // MOE_SUB + 1):
            nrows = nsub * MOE_SUB

            @pl.when(nsub_ref[g] == nsub)
            def _():
                @pl.when(f == 0)
                def _():
                    _gather_start(slot_ref, g * MOE_ROWS, 1, h_hbm, y_ref, nrows, sem, priorities=2)
                    _gather_wait(h_hbm, y_ref, nrows, sem)
                    xb_scr[0:nrows, :] = y_ref[0:nrows, :].astype(BF16)
                    y_ref[...] = jnp.zeros_like(y_ref)

                x = xb_scr[0:nrows, :]
                a = lax.dot_general(x, w1_ref[0], _NN, preferred_element_type=F32)
                b = lax.dot_general(x, w3_ref[0], _NN, preferred_element_type=F32)
                u = (a * jax.nn.sigmoid(a) * b).astype(BF16)
                y_ref[0:nrows, :] += lax.dot_general(u, w2_ref[0], _NN, preferred_element_type=F32)


def _experts(h, slot_tok, blk_e, nsub, nb_used, w1, w3, w2, tf=512):
    n, d = h.shape
    ff = w1.shape[2]
    nf = ff // tf
    n_blocks = slot_tok.shape[0] // MOE_ROWS

    def gidx(g, nb):
        return jnp.minimum(g, nb[0] - 1)

    def fidx(g, f, nb):
        return jnp.where(g < nb[0], f, nf - 1)

    grid_spec = pltpu.PrefetchScalarGridSpec(
        num_scalar_prefetch=4,
        grid=(n_blocks, nf),
        in_specs=[pl.BlockSpec(memory_space=pl.ANY),
                  pl.BlockSpec((1, d, tf), lambda g, f, be, ns, nb, st: (be[gidx(g, nb)], 0, fidx(g, f, nb))),
                  pl.BlockSpec((1, d, tf), lambda g, f, be, ns, nb, st: (be[gidx(g, nb)], 0, fidx(g, f, nb))),
                  pl.BlockSpec((1, tf, d), lambda g, f, be, ns, nb, st: (be[gidx(g, nb)], fidx(g, f, nb), 0))],
        out_specs=pl.BlockSpec((MOE_ROWS, d), lambda g, f, be, ns, nb, st: (g, 0)),
        scratch_shapes=[pltpu.VMEM((MOE_ROWS, d), BF16),
                        pltpu.SemaphoreType.DMA],
    )
    return pl.pallas_call(
        _experts_kernel,
        out_shape=jax.ShapeDtypeStruct((n_blocks * MOE_ROWS, d), F32),
        grid_spec=grid_spec,
        compiler_params=_cparams(("arbitrary", "arbitrary")),
        name="moe_experts",
    )(blk_e, nsub, nb_used, slot_tok, h, w1, w3, w2)


def _combine_kernel(dest_ref, x_ref, mod_ref, w_ref, gf_ref, y_hbm, o_ref, y0_scr, y1_scr, sem):
    tm = x_ref.shape[0]
    base = pl.program_id(0) * tm * TOP_K
    _gather_start(dest_ref, base, TOP_K, y_hbm, y0_scr, tm, sem.at[0], priorities=2)
    _gather_start(dest_ref, base + 1, TOP_K, y_hbm, y1_scr, tm, sem.at[1], priorities=2)
    _gather_wait(y_hbm, y0_scr, tm, sem.at[0])
    _gather_wait(y_hbm, y1_scr, tm, sem.at[1])
    w = w_ref[...]
    y = w[:, 0:1] * y0_scr[...] + w[:, 1:2] * y1_scr[...]
    x = x_ref[...] + mod_ref[0, 5:6, :] * y
    o_ref[...] = x * lax.rsqrt(jnp.mean(x * x, axis=-1, keepdims=True) + EPS) * gf_ref[...]


def _combine_final(tok, mod, y, dest, top_w, g_final, layout, skip_rows):
    rows, d = tok.shape
    tm = ROW_TILE
    off = skip_rows // tm
    nrow = rows - skip_rows
    grid_spec = pltpu.PrefetchScalarGridSpec(
        num_scalar_prefetch=1,
        grid=(nrow // tm,),
        in_specs=[pl.BlockSpec((tm, d), lambda i, ds: (i + off, 0)),
                  pl.BlockSpec((1, 6, d), lambda i, ds: (_mod_row(i + off, tm, layout), 0, 0)),
                  pl.BlockSpec((tm, TOP_K), lambda i, ds: (i, 0)),
                  pl.BlockSpec((1, d), lambda i, ds: (0, 0)),
                  pl.BlockSpec(memory_space=pl.ANY)],
        out_specs=pl.BlockSpec((tm, d), lambda i, ds: (i, 0)),
        scratch_shapes=[pltpu.VMEM((tm, d), F32), pltpu.VMEM((tm, d), F32),
                        pltpu.SemaphoreType.DMA((TOP_K,))],
    )
    return pl.pallas_call(
        _combine_kernel,
        out_shape=jax.ShapeDtypeStruct((nrow, d), F32),
        grid_spec=grid_spec,
        compiler_params=_cparams(("arbitrary",)),
        name="moe_combine_final",
    )(dest, tok, mod, top_w, g_final.reshape(1, d), y)


def _route(logits):
    n = logits.shape[0]
    na = n * TOP_K
    top_logit, top_e = lax.top_k(logits, TOP_K)
    top_w = jax.nn.softmax(top_logit, axis=-1)
    a_e = top_e.reshape(-1).astype(I32)
    order = jnp.argsort(a_e, stable=True).astype(I32)
    inv = jnp.argsort(order).astype(I32)
    counts = jnp.sum(a_e[:, None] == jnp.arange(N_EXPERTS, dtype=I32)[None, :], axis=0).astype(I32)
    padded = (counts + MOE_ROWS - 1) // MOE_ROWS * MOE_ROWS
    pad_end = jnp.cumsum(padded)
    pad_start = pad_end - padded
    start = jnp.cumsum(counts) - counts
    dest = inv + (pad_start - start)[a_e]

    n_blocks = na // MOE_ROWS + N_EXPERTS
    slot = jnp.arange(n_blocks * MOE_ROWS, dtype=I32)
    slot_e = jnp.minimum(jnp.sum(slot[:, None] >= pad_end[None, :], axis=1), N_EXPERTS - 1).astype(I32)
    rank = slot - pad_start[slot_e]
    src = jnp.clip(start[slot_e] + rank, 0, na - 1)
    slot_tok = jnp.where(rank < counts[slot_e], order[src] // TOP_K, slot % n)

    blk0 = jnp.arange(n_blocks, dtype=I32) * MOE_ROWS
    blk_e = slot_e[::MOE_ROWS]
    blk_rows = jnp.clip(counts[blk_e] - (blk0 - pad_start[blk_e]), 0, MOE_ROWS)
    nb_used = (pad_end[-1:] // MOE_ROWS).astype(I32)
    nsub = jnp.where(blk0 < pad_end[-1], (blk_rows + MOE_SUB - 1) // MOE_SUB, 0).astype(I32)
    return top_w, slot_tok, dest, blk_e, nsub, nb_used


def kernel(x, c, ctx, c_ctx, w_ada, b_ada, g_mix, w_in, conv_w, conv_b, conv_ln_g, conv_ln_b, hgrn_lb_logits, hgrn_norm_g, w_out, g_ffn, ffn_w1, ffn_w3, ffn_w2, router_w, moe_w1, moe_w3, moe_w2, g_final):
    batch, seq, d = x.shape
    ctx_len = ctx.shape[1]
    depth = w_ada.shape[0]
    assert depth == 2 and batch == 2
    conv_ch = conv_w.shape[2]
    ctx_rows = batch * ctx_len
    layout = (ctx_rows, seq)

    cs = jnp.cumsum(jax.nn.softmax(hgrn_lb_logits.astype(F32), axis=1), axis=1)
    lb_all = cs - cs[:, :1]

    cc = jnp.zeros((MOD_ROWS, d), F32).at[0:batch].set(c).at[batch].set(c_ctx)
    mod_all = _adaln(cc, w_ada, b_ada).reshape(depth, MOD_ROWS, 6, d)

    tok = jnp.concatenate([ctx.reshape(ctx_rows, d), x.reshape(batch * seq, d)], axis=0)
    row = lambda a: a.reshape(1, -1)
    w_in_b = w_in.astype(BF16)
    w_out_b = w_out.astype(BF16)

    for l in range(depth):
        last = l == depth - 1
        mod = mod_all[l]
        px = _in_proj(tok, mod, row(g_mix[l]), w_in_b, l, layout)
        o_f, o_b = _hgrn_scan(px, lb_all[:, l], batch, ctx_len, seq, 2 * conv_ch)
        conv = _conformer_conv(px, conv_w[l], conv_b[l], conv_ln_g[l], conv_ln_b[l], ctx_len,
                               with_ctx=not last, skip_blocks=ctx_rows // ctx_len)
        skip_rows = ctx_rows if last else 0
        tok = _out_proj(tok, mod, conv, o_f, o_b, px, hgrn_norm_g[l], w_out_b, l, layout, skip_rows)
        j = l // 2
        if l % 2 == 0:
            tok = _dense_ffn(tok, mod, row(g_ffn[l]), ffn_w1[j].astype(BF16), ffn_w3[j].astype(BF16),
                             ffn_w2[j].astype(BF16), layout)
        else:
            assert last
            wr_pad = jnp.zeros((d, 128), F32).at[:, :N_EXPERTS].set(router_w[j])
            lat_layout = (0, seq)
            h, logits = _router(tok, mod, row(g_ffn[l]), wr_pad, lat_layout, 0)
            top_w, slot_tok, dest, blk_e, nsub, nb_used = _route(logits[:, :N_EXPERTS])
            y = _experts(h, slot_tok, blk_e, nsub, nb_used, moe_w1[j], moe_w3[j], moe_w2[j])
            out = _combine_final(tok, mod, y, dest, top_w, g_final, lat_layout, 0)
    return out.reshape(batch, seq, d)
```
